```python
import math
import jax, jax.numpy as jnp
from jax import lax
import numpy as np

D_MODEL = 1024
BATCH = 16
SEQ = 4096
DEPTH = 4
DEC_BATCH = 8
DEC_SEQ = 4096
PAST_LEN = 128

GRID_W = 64
QBLK = 128
D_PLE = 256
EPS = 1e-6

H_A = 4
DH_A = 64
W_A = H_A * 2 * DH_A
H_B = 8
KV_B = 2
REP_B = H_B // KV_B
DH_B = 64
W_B = H_B * DH_B
ROPE_THETA = 10000.0
H_C = 8
DH_C = 64
W_C = H_C * DH_C
NA_ROWS = 8
NA_COLS = 16
T5_BUCKETS = 32
T5_MAX_DIST = 128

IN_SIZES = (
    2 * H_A * DH_A, 2 * H_A * DH_A, W_A, W_A,
    W_B, KV_B * DH_B, KV_B * DH_B, W_B,
    W_C, W_C, W_C, W_C,
    3 * D_MODEL,
)
IN_W = sum(IN_SIZES)

kernel_name = "hybrid_diff_gqa_natten_encoder"


def rms_norm(x, g):
    x32 = x.astype(jnp.float32)
    y = x32 * lax.rsqrt(jnp.mean(x32 * x32, axis=-1, keepdims=True) + EPS)
    return (y * g.astype(jnp.float32)).astype(x.dtype)


def to_blocks(t):
    b, s = t.shape[:2]
    return jnp.moveaxis(t.reshape(b, s // QBLK, QBLK, *t.shape[2:]), 1, 0)


def from_blocks(t):
    nb, b, q = t.shape[:3]
    return jnp.moveaxis(t, 0, 1).reshape(b, nb * q, *t.shape[3:])


def t5_bucket(rel):
    nb = T5_BUCKETS // 2
    ret = jnp.where(rel > 0, nb, 0)
    n = jnp.abs(rel)
    max_exact = nb // 2
    nf = jnp.maximum(n, 1).astype(jnp.float32)
    large = max_exact + (jnp.log(nf / max_exact) / math.log(T5_MAX_DIST / max_exact)
                         * (nb - max_exact)).astype(jnp.int32)
    large = jnp.minimum(large, nb - 1)
    return ret + jnp.where(n < max_exact, n, large)


def axial_rope_tables(s):
    t = jnp.arange(s)
    row = (t // GRID_W).astype(jnp.float32)
    col = (t % GRID_W).astype(jnp.float32)
    n_freq = DH_B // 4
    inv = ROPE_THETA ** (-jnp.arange(n_freq, dtype=jnp.float32) / n_freq)
    ang = jnp.stack([row[:, None] * inv, col[:, None] * inv], axis=1)
    ang = jnp.stack([ang, ang], axis=2)
    return jnp.cos(ang), jnp.sin(ang)


def apply_axial_rope(x, cos, sin):
    shp = x.shape
    xr = x.reshape(*shp[:-1], 2, 2, DH_B // 4)
    rot = jnp.stack([-xr[..., 1, :], xr[..., 0, :]], axis=-2)
    c = cos[None, :, None].astype(x.dtype)
    s = sin[None, :, None].astype(x.dtype)
    return (xr * c + rot * s).reshape(shp)


def diff_attention(q, k, v, lam, lambda_init, subln_g, rel_bias):
    b, s = q.shape[:2]
    scale = DH_A ** -0.5
    kpos = jnp.arange(s)

    def block(args):
        qb, bi = args
        qpos = bi * QBLK + jnp.arange(QBLK)
        bias = rel_bias[kpos[None, :] - qpos[:, None] + s - 1]
        bias = jnp.transpose(bias, (2, 3, 0, 1))
        logits = jnp.einsum('bqhmd,bkhmd->bhmqk', qb, k).astype(jnp.float32) * scale + bias
        p = jax.nn.softmax(logits, axis=-1)
        attn = (p[:, :, 0] - lam * p[:, :, 1]).astype(v.dtype)
        return jnp.einsum('bhqk,bkhe->bqhe', attn, v)

    o = from_blocks(lax.map(block, (to_blocks(q), jnp.arange(s // QBLK))))
    o = rms_norm(o, subln_g) * (1.0 - lambda_init)
    return o.reshape(b, s, W_A)


def gqa_attention(q, k, v):
    b, s = q.shape[:2]
    scale = DH_B ** -0.5

    def block(qb):
        logits = jnp.einsum('bqgrd,bkgd->bgrqk', qb, k).astype(jnp.float32) * scale
        p = jax.nn.softmax(logits, axis=-1).astype(v.dtype)
        return jnp.einsum('bgrqk,bkgd->bqgrd', p, v)

    o = from_blocks(lax.map(block, to_blocks(q)))
    return o.reshape(b, s, W_B)


def neighbourhood_attention(q, k, v, rpb):
    b, s = q.shape[:2]
    rows = s // GRID_W
    wr = min(NA_ROWS, rows)
    band = min(wr + QBLK // GRID_W - 1, rows)
    scale = DH_C ** -0.5
    kg = k.reshape(b, rows, GRID_W, H_C, DH_C)
    vg = v.reshape(b, rows, GRID_W, H_C, DH_C)
    kc = jnp.tile(jnp.arange(GRID_W), band)
    kr_off = jnp.repeat(jnp.arange(band), GRID_W)

    def block(args):
        qb, bi = args
        t = bi * QBLK + jnp.arange(QBLK)
        qr = t // GRID_W
        qc = t % GRID_W
        sr = jnp.clip(qr - wr // 2, 0, rows - wr)
        sc = jnp.clip(qc - NA_COLS // 2, 0, GRID_W - NA_COLS)
        b0 = jnp.clip(sr[0], 0, rows - band)
        kb = lax.dynamic_slice_in_dim(kg, b0, band, axis=1).reshape(b, band * GRID_W, H_C, DH_C)
        vb = lax.dynamic_slice_in_dim(vg, b0, band, axis=1).reshape(b, band * GRID_W, H_C, DH_C)
        kr = b0 + kr_off
        mask = ((kr[None] >= sr[:, None]) & (kr[None] < (sr + wr)[:, None])
                & (kc[None] >= sc[:, None]) & (kc[None] < (sc + NA_COLS)[:, None]))
        ri = jnp.clip(kr[None] - qr[:, None] + NA_ROWS - 1, 0, 2 * NA_ROWS - 2)
        ci = jnp.clip(kc[None] - qc[:, None] + NA_COLS - 1, 0, 2 * NA_COLS - 2)
        bias = rpb[:, ri, ci].astype(jnp.float32)
        logits = jnp.einsum('bqhd,bkhd->bhqk', qb, kb).astype(jnp.float32) * scale + bias
        logits = jnp.where(mask, logits, -jnp.inf)
        p = jax.nn.softmax(logits, axis=-1).astype(vb.dtype)
        return jnp.einsum('bhqk,bkhd->bqhd', p, vb)

    o = from_blocks(lax.map(block, (to_blocks(q), jnp.arange(s // QBLK))))
    return o.reshape(b, s, W_C)


def mixer_layer(x, p_i, layer_idx, pre_g, w_in, lam_p, subln_g, qn_g, kn_g, rpb,
                wa, wb, wc, w_out, post_g, w_pe, w_pg, rel_bias, rope_cos, rope_sin):
    b, s, _ = x.shape
    h = rms_norm(x, pre_g)
    u = h @ w_in
    split_points = np.cumsum(IN_SIZES)[:-1].tolist()
    aq, ak, av, ag, bq, bk, bv, bg, cq, ck, cv, cg, mg = jnp.split(u, split_points, axis=-1)

    lambda_init = 0.8 - 0.6 * math.exp(-0.3 * layer_idx)
    lp = lam_p.astype(jnp.float32)
    lam = jnp.exp(jnp.sum(lp[0] * lp[1])) - jnp.exp(jnp.sum(lp[2] * lp[3])) + lambda_init
    ya = diff_attention(aq.reshape(b, s, H_A, 2, DH_A), ak.reshape(b, s, H_A, 2, DH_A),
                        av.reshape(b, s, H_A, 2 * DH_A), lam, lambda_init, subln_g, rel_bias)
    ya = ya * jax.nn.silu(ag)

    qh = apply_axial_rope(rms_norm(bq.reshape(b, s, H_B, DH_B), qn_g), rope_cos, rope_sin)
    kh = apply_axial_rope(rms_norm(bk.reshape(b, s, KV_B, DH_B), kn_g), rope_cos, rope_sin)
    yb = gqa_attention(qh.reshape(b, s, KV_B, REP_B, DH_B), kh, bv.reshape(b, s, KV_B, DH_B))
    yb = yb * jax.nn.silu(bg)

    yc = neighbourhood_attention(cq.reshape(b, s, H_C, DH_C), ck.reshape(b, s, H_C, DH_C),
                                 cv.reshape(b, s, H_C, DH_C), rpb)
    yc = yc * jax.nn.silu(cg)

    gates = jax.nn.sigmoid(mg).reshape(b, s, 3, D_MODEL)
    m = gates[:, :, 0] * (ya @ wa) + gates[:, :, 1] * (yb @ wb) + gates[:, :, 2] * (yc @ wc)
    x = x + rms_norm(m @ w_out, post_g)

    x = x + jax.nn.sigmoid(x @ w_pg) * (p_i @ w_pe)
    return x


def trunk(x, p, t5_table, pre_norm_g, w_in, lambda_qk, subln_g, q_norm_g, k_norm_g, na_rpb,
          w_branch_a, w_branch_b, w_branch_c, w_out, post_norm_g, w_ple_proj, w_ple_gate):
    s = x.shape[1]
    rel = jnp.arange(-(s - 1), s)
    rel_bias = t5_table[t5_bucket(rel)].astype(jnp.float32).reshape(2 * s - 1, H_A, 2)
    rope_cos, rope_sin = axial_rope_tables(s)
    for i in range(DEPTH):
        x = mixer_layer(x, p[i], i, pre_norm_g[i], w_in[i], lambda_qk[i], subln_g[i],
                        q_norm_g[i], k_norm_g[i], na_rpb[i], w_branch_a[i], w_branch_b[i],
                        w_branch_c[i], w_out[i], post_norm_g[i], w_ple_proj[i], w_ple_gate[i],
                        rel_bias, rope_cos, rope_sin)
    return x


def setup_inputs(seed: int = 0) -> dict:
    key = jax.random.key(seed)
    ks = jax.random.split(key, 20)
    f32 = jnp.float32

    def nrm(k, shape, scale):
        return jax.random.normal(k, shape, f32) * scale

    return {
        "x_prompt": nrm(ks[0], (BATCH, SEQ, D_MODEL), 1.0),
        "x_sample": nrm(ks[1], (DEC_BATCH, DEC_SEQ, D_MODEL), 1.0),
        "p_prompt": nrm(ks[2], (DEPTH, BATCH, SEQ, D_PLE), 1.0),
        "p_sample": nrm(ks[3], (DEPTH, DEC_BATCH, DEC_SEQ, D_PLE), 1.0),
        "t5_table": nrm(ks[4], (T5_BUCKETS, 2 * H_A), 0.2),
        "pre_norm_g": 1.0 + nrm(ks[5], (DEPTH, D_MODEL), 0.1),
        "w_in": nrm(ks[6], (DEPTH, D_MODEL, IN_W), D_MODEL ** -0.5),
        "lambda_qk": nrm(ks[7], (DEPTH, 4, DH_A), 0.1),
        "subln_g": 1.0 + nrm(ks[8], (DEPTH, 2 * DH_A), 0.1),
        "q_norm_g": 1.0 + nrm(ks[9], (DEPTH, DH_B), 0.1),
        "k_norm_g": 1.0 + nrm(ks[10], (DEPTH, DH_B), 0.1),
        "na_rpb": nrm(ks[11], (DEPTH, H_C, 2 * NA_ROWS - 1, 2 * NA_COLS - 1), 0.2),
        "w_branch_a": nrm(ks[12], (DEPTH, W_A, D_MODEL), W_A ** -0.5),
        "w_branch_b": nrm(ks[13], (DEPTH, W_B, D_MODEL), W_B ** -0.5),
        "w_branch_c": nrm(ks[14], (DEPTH, W_C, D_MODEL), W_C ** -0.5),
        "w_out": nrm(ks[15], (DEPTH, D_MODEL, D_MODEL), D_MODEL ** -0.5),
        "post_norm_g": 1.0 + nrm(ks[16], (DEPTH, D_MODEL), 0.1),
        "w_ple_proj": nrm(ks[17], (DEPTH, D_PLE, D_MODEL), D_PLE ** -0.5),
        "w_ple_gate": nrm(ks[18], (DEPTH, D_MODEL, D_MODEL), D_MODEL ** -0.5),
    }


def reference(x_prompt, x_sample, p_prompt, p_sample, t5_table, pre_norm_g, w_in, lambda_qk,
              subln_g, q_norm_g, k_norm_g, na_rpb, w_branch_a, w_branch_b, w_branch_c, w_out,
              post_norm_g, w_ple_proj, w_ple_gate):
    y_prompt = trunk(x_prompt, p_prompt, t5_table, pre_norm_g, w_in, lambda_qk, subln_g,
                     q_norm_g, k_norm_g, na_rpb, w_branch_a, w_branch_b, w_branch_c, w_out,
                     post_norm_g, w_ple_proj, w_ple_gate)
    y_sample = trunk(x_sample, p_sample, t5_table, pre_norm_g, w_in, lambda_qk, subln_g,
                     q_norm_g, k_norm_g, na_rpb, w_branch_a, w_branch_b, w_branch_c, w_out,
                     post_norm_g, w_ple_proj, w_ple_gate)
    return (y_prompt, y_sample)
```

```python
import functools
import math

import numpy as np
import jax
import jax.numpy as jnp
from jax import lax
from jax.experimental import pallas as pl
from jax.experimental.pallas import tpu as pltpu

F32 = jnp.float32
BF16 = jnp.bfloat16

D_MODEL = 1024
D_PLE = 256
EPS = 1e-6
GRID_W = 64
LANES = 128
DH = 64
H_A = 4
H_B = 8
KV_B = 2
REP_B = H_B // KV_B
H_C = 8
W_BR = 512
NA_ROWS = 8
NA_COLS = 16
NA_BAND = 10
NA_QBLK = 2 * GRID_W
T5_BUCKETS = 32
T5_MAX_DIST = 128
ROPE_THETA = 10000.0
QK_SCALE = DH ** -0.5

OFF_MG = 0
OFF_AQ, OFF_AK, OFF_AV, OFF_AG = 3072, 3584, 4096, 4608
OFF_BQ, OFF_BG = 5120, 5632
OFF_CQ, OFF_CK, OFF_CV, OFF_CG = 6144, 6656, 7168, 7680
OFF_BK, OFF_BV = 8192, 8320
IN_W = 8448

VMEM_LIMIT = 56 * 1024 * 1024


def _b_head_perm():
    return np.array([(p % KV_B) * REP_B + p // KV_B for p in range(H_B)])


def _in_proj_column_perm():
    o = {}
    off = 0
    for name, size in (("aq", 512), ("ak", 512), ("av", 512), ("ag", 512), ("bq", 512), ("bk", 128),
                       ("bv", 128), ("bg", 512), ("cq", 512), ("ck", 512), ("cv", 512), ("cg", 512),
                       ("mg", 3072)):
        o[name] = np.arange(off, off + size)
        off += size
    hp = _b_head_perm()
    head_cols = (hp[:, None] * DH + np.arange(DH)[None, :]).reshape(-1)
    bq = o["bq"][head_cols]
    bg = o["bg"][head_cols]
    return np.concatenate([o["mg"], o["aq"], o["ak"], o["av"], o["ag"], bq, bg,
                           o["cq"], o["ck"], o["cv"], o["cg"], o["bk"], o["bv"]])


def _silu(x):
    return x * jax.nn.sigmoid(x)


def _dot_nt(a, b):
    return lax.dot_general(a, b, (((1,), (1,)), ((), ())), preferred_element_type=F32)


def _dot(a, b):
    return jnp.dot(a, b, preferred_element_type=F32)


def _pair_norm_rope(a, gain, cos, sin_lo, sin_hi, left):
    sq = a * a
    s_l = jnp.sum(jnp.where(left, sq, 0.0), axis=-1, keepdims=True)
    s_r = jnp.sum(jnp.where(left, 0.0, sq), axis=-1, keepdims=True)
    ms = jnp.where(left, s_l, s_r) * (1.0 / DH)
    y = a * lax.rsqrt(ms + EPS) * gain
    return y * cos + pltpu.roll(y, 112, 1) * sin_lo + pltpu.roll(y, 16, 1) * sin_hi


def _in_proj_kernel(x_ref, g_ref, w_ref, qn_ref, kn_ref, cos_ref, slo_ref, shi_ref, u_ref):
    x = x_ref[...]
    ms = jnp.mean(x * x, axis=-1, keepdims=True)
    h = (x * lax.rsqrt(ms + EPS) * g_ref[...]).astype(BF16)
    tm = x.shape[0]
    left = lax.broadcasted_iota(jnp.int32, (tm, LANES), 1) < DH
    cos, slo, shi = cos_ref[...], slo_ref[...], shi_ref[...]
    for c0 in range(0, OFF_BK, 512):
        acc = _dot(h, w_ref[:, c0:c0 + 512])
        if c0 == OFF_BQ:
            for b in range(4):
                blk = _pair_norm_rope(acc[:, b * LANES:(b + 1) * LANES], qn_ref[...], cos, slo, shi, left)
                u_ref[:, c0 + b * LANES:c0 + (b + 1) * LANES] = (blk * QK_SCALE).astype(BF16)
        else:
            u_ref[:, c0:c0 + 512] = acc.astype(BF16)
    acc = _dot(h, w_ref[:, OFF_BK:IN_W])
    bk = _pair_norm_rope(acc[:, :LANES], kn_ref[...], cos, slo, shi, left)
    u_ref[:, OFF_BK:OFF_BV] = bk.astype(BF16)
    u_ref[:, OFF_BV:IN_W] = acc[:, LANES:].astype(BF16)


def _in_proj(x, pre_g, w_in, qn, kn, rope, layer, seq, tm):
    n = x.shape[0]
    nseq = seq // tm
    cos, slo, shi = rope
    vec = lambda width: pl.BlockSpec((None, 1, width), lambda t: (layer, 0, 0))
    tab = pl.BlockSpec((tm, LANES), lambda t: (t % nseq, 0))
    return pl.pallas_call(
        _in_proj_kernel,
        grid=(n // tm,),
        in_specs=[
            pl.BlockSpec((tm, D_MODEL), lambda t: (t, 0)),
            vec(D_MODEL),
            pl.BlockSpec((None, D_MODEL, IN_W), lambda t: (layer, 0, 0), pipeline_mode=pl.Buffered(1)),
            vec(LANES), vec(LANES), tab, tab, tab,
        ],
        out_specs=pl.BlockSpec((tm, IN_W), lambda t: (t, 0)),
        out_shape=jax.ShapeDtypeStruct((n, IN_W), BF16),
        compiler_params=pltpu.CompilerParams(dimension_semantics=("arbitrary",),
                                             vmem_limit_bytes=VMEM_LIMIT),
    )(x, pre_g, w_in, qn, kn, cos, slo, shi)


def _online_step(s, v, m, l, acc):
    m_new = jnp.maximum(m, jnp.max(s, axis=-1, keepdims=True))
    alpha = jnp.exp(m - m_new)
    p = jnp.exp(s - m_new)
    l = alpha * l + jnp.sum(p, axis=-1, keepdims=True)
    acc = alpha * acc + _dot(p.astype(BF16), v)
    return m_new, l, acc


def _online_init(tq):
    return (jnp.full((tq, 1), -jnp.inf, F32), jnp.zeros((tq, 1), F32), jnp.zeros((tq, LANES), F32))


def _diff_attn_kernel(lam_ref, sg_ref, q_ref, k_ref, v_ref, g_ref, t_ref, o_ref, *, lambda_init, tk, nk):
    i = pl.program_id(1)
    tq = q_ref.shape[0]
    lp = lam_ref[...]
    lam = (jnp.exp(jnp.sum(lp[0:1] * lp[1:2], keepdims=True))
           - jnp.exp(jnp.sum(lp[2:3] * lp[3:4], keepdims=True)) + lambda_init)
    left = lax.broadcasted_iota(jnp.int32, (tq, LANES), 1) < DH
    for h in range(H_A):
        cs = slice(h * LANES, (h + 1) * LANES)
        qh = q_ref[:, cs] * QK_SCALE
        zero = jnp.zeros_like(qh)
        q0 = jnp.where(left, qh, zero)
        q1 = jnp.where(left, zero, qh)

        def body(j, carry, cs=cs, q0=q0, q1=q1, h=h):
            st0, st1 = carry
            ks = pl.ds(pl.multiple_of(j * tk, tk), tk)
            kj = k_ref[ks, cs]
            vj = v_ref[ks, cs]
            d = jnp.clip(j - i, -2, 2) + 2
            s0 = _dot_nt(q0, kj) + t_ref[d, 2 * h]
            s1 = _dot_nt(q1, kj) + t_ref[d, 2 * h + 1]
            return _online_step(s0, vj, *st0), _online_step(s1, vj, *st1)

        (_, l0, a0), (_, l1, a1) = lax.fori_loop(0, nk, body, (_online_init(tq), _online_init(tq)))
        o = a0 / l0 - lam * (a1 / l1)
        o = o * lax.rsqrt(jnp.mean(o * o, axis=-1, keepdims=True) + EPS) * sg_ref[...]
        o = o * (1.0 - lambda_init)
        o_ref[:, cs] = (o * _silu(g_ref[:, cs].astype(F32))).astype(BF16)


def _diff_attn(u, lam_qk, subln_g, t5_tiles, layer, tq):
    b, seq, _ = u.shape
    tk = tq
    lambda_init = 0.8 - 0.6 * math.exp(-0.3 * layer)
    col = lambda off: off // W_BR
    row_blk = lambda off: pl.BlockSpec((None, tq, W_BR), lambda bi, i: (bi, i, col(off)))
    seq_blk = lambda off: pl.BlockSpec((None, seq, W_BR), lambda bi, i: (bi, 0, col(off)))
    return pl.pallas_call(
        functools.partial(_diff_attn_kernel, lambda_init=lambda_init, tk=tk, nk=seq // tk),
        grid=(b, seq // tq),
        in_specs=[
            pl.BlockSpec((None, 4, DH), lambda bi, i: (layer, 0, 0)),
            pl.BlockSpec((None, 1, LANES), lambda bi, i: (layer, 0, 0)),
            row_blk(OFF_AQ), seq_blk(OFF_AK), seq_blk(OFF_AV), row_blk(OFF_AG),
            pl.BlockSpec(t5_tiles.shape, lambda bi, i: (0, 0, 0, 0), pipeline_mode=pl.Buffered(1)),
        ],
        out_specs=pl.BlockSpec((None, tq, W_BR), lambda bi, i: (bi, i, 0)),
        out_shape=jax.ShapeDtypeStruct((b, seq, W_BR), BF16),
        compiler_params=pltpu.CompilerParams(dimension_semantics=("arbitrary", "arbitrary"),
                                             vmem_limit_bytes=VMEM_LIMIT),
    )(lam_qk, subln_g, u, u, u, u, t5_tiles)


def _gqa_kernel(q_ref, k_ref, v_ref, g_ref, o_ref, *, tk, nk):
    tq = q_ref.shape[0]
    left = lax.broadcasted_iota(jnp.int32, (tq, LANES), 1) < DH
    for pair in range(REP_B):
        cs = slice(pair * LANES, (pair + 1) * LANES)
        qp = q_ref[:, cs]
        zero = jnp.zeros_like(qp)
        q0 = jnp.where(left, qp, zero)
        q1 = jnp.where(left, zero, qp)

        def body(j, carry, q0=q0, q1=q1):
            st0, st1 = carry
            ks = pl.ds(pl.multiple_of(j * tk, tk), tk)
            kj = k_ref[ks, :]
            vj = v_ref[ks, :]
            return (_online_step(_dot_nt(q0, kj), vj, *st0), _online_step(_dot_nt(q1, kj), vj, *st1))

        (_, l0, a0), (_, l1, a1) = lax.fori_loop(0, nk, body, (_online_init(tq), _online_init(tq)))
        o = jnp.where(left, a0 / l0, a1 / l1)
        o_ref[:, cs] = (o * _silu(g_ref[:, cs].astype(F32))).astype(BF16)


def _gqa_attn(u, tq, tk):
    b, seq, _ = u.shape
    row_blk = lambda off: pl.BlockSpec((None, tq, W_BR), lambda bi, i: (bi, i, off // W_BR))
    seq_blk = lambda off: pl.BlockSpec((None, seq, LANES), lambda bi, i: (bi, 0, off // LANES))
    return pl.pallas_call(
        functools.partial(_gqa_kernel, tk=tk, nk=seq // tk),
        grid=(b, seq // tq),
        in_specs=[row_blk(OFF_BQ), seq_blk(OFF_BK), seq_blk(OFF_BV), row_blk(OFF_BG)],
        out_specs=pl.BlockSpec((None, tq, W_BR), lambda bi, i: (bi, i, 0)),
        out_shape=jax.ShapeDtypeStruct((b, seq, W_BR), BF16),
        compiler_params=pltpu.CompilerParams(dimension_semantics=("arbitrary", "arbitrary"),
                                             vmem_limit_bytes=VMEM_LIMIT),
    )(u, u, u, u)


def _na_block_type(bi, nblk):
    return jnp.where(bi < 2, bi, jnp.where(bi <= nblk - 3, 2, bi - (nblk - 5)))


def _na_kernel(q_ref, k_ref, v_ref, g_ref, tab_ref, o_ref, *, rows):
    big = pl.program_id(1)
    nsub = q_ref.shape[0] // NA_QBLK
    nblk = rows // 2
    nkeys = NA_BAND * GRID_W
    left = lax.broadcasted_iota(jnp.int32, (NA_QBLK, LANES), 1) < DH
    for sub in range(nsub):
        bi = big * nsub + sub
        b0 = jnp.clip(2 * bi - NA_ROWS // 2, 0, rows - NA_BAND)
        typ = _na_block_type(bi, nblk)
        ks = pl.ds(pl.multiple_of(b0 * GRID_W, GRID_W), nkeys)
        rs = slice(sub * NA_QBLK, (sub + 1) * NA_QBLK)
        for pair in range(H_C // 2):
            cs = slice(pair * LANES, (pair + 1) * LANES)
            kb = k_ref[ks, cs]
            vb = v_ref[ks, cs]
            qp = q_ref[rs, cs] * QK_SCALE
            zero = jnp.zeros_like(qp)
            outs = []
            for e in range(2):
                qm = jnp.where(left, qp, zero) if e == 0 else jnp.where(left, zero, qp)
                s = _dot_nt(qm, kb) + tab_ref[typ, 2 * pair + e]
                m = jnp.max(s, axis=-1, keepdims=True)
                p = jnp.exp(s - m)
                l = jnp.sum(p, axis=-1, keepdims=True)
                outs.append(_dot(p.astype(BF16), vb) / l)
            o = jnp.where(left, outs[0], outs[1])
            o_ref[rs, cs] = (o * _silu(g_ref[rs, cs].astype(F32))).astype(BF16)


def _na_attn(u, na_tab, layer, tq):
    b, seq, _ = u.shape
    rows = seq // GRID_W
    row_blk = lambda off: pl.BlockSpec((None, tq, W_BR), lambda bi, i: (bi, i, off // W_BR))
    seq_blk = lambda off: pl.BlockSpec((None, seq, W_BR), lambda bi, i: (bi, 0, off // W_BR))
    return pl.pallas_call(
        functools.partial(_na_kernel, rows=rows),
        grid=(b, seq // tq),
        in_specs=[
            row_blk(OFF_CQ), seq_blk(OFF_CK), seq_blk(OFF_CV), row_blk(OFF_CG),
            pl.BlockSpec((None,) + na_tab.shape[1:], lambda bi, i: (layer, 0, 0, 0, 0),
                         pipeline_mode=pl.Buffered(1)),
        ],
        out_specs=pl.BlockSpec((None, tq, W_BR), lambda bi, i: (bi, i, 0)),
        out_shape=jax.ShapeDtypeStruct((b, seq, W_BR), BF16),
        compiler_params=pltpu.CompilerParams(dimension_semantics=("arbitrary", "arbitrary"),
                                             vmem_limit_bytes=VMEM_LIMIT),
    )(u, u, u, u, na_tab)


def _merge_kernel(x_ref, ya_ref, yb_ref, yc_ref, mg_ref, p_ref, wa_ref, wb_ref, wc_ref, wo_ref,
                  wpg_ref, wpe_ref, pg_ref, o_ref):
    def gate(k):
        return jax.nn.sigmoid(mg_ref[:, k * D_MODEL:(k + 1) * D_MODEL].astype(F32))

    m = gate(0) * _dot(ya_ref[...], wa_ref[...])
    m = m + gate(1) * _dot(yb_ref[...], wb_ref[...])
    m = m + gate(2) * _dot(yc_ref[...], wc_ref[...])
    r = _dot(m.astype(BF16), wo_ref[...])
    r = r * lax.rsqrt(jnp.mean(r * r, axis=-1, keepdims=True) + EPS) * pg_ref[...]
    x = x_ref[...] + r
    emb = _dot(p_ref[...].astype(BF16), wpe_ref[...])
    o_ref[...] = x + jax.nn.sigmoid(_dot(x.astype(BF16), wpg_ref[...])) * emb


def _merge(x, ya, yb, yc, u, p, wa, wb, wc, wo, wpg, wpe, post_g, layer, tm):
    n = x.shape[0]
    tok = lambda width: pl.BlockSpec((tm, width), lambda t: (t, 0))
    wgt = lambda r, c: pl.BlockSpec((None, r, c), lambda t: (layer, 0, 0))
    return pl.pallas_call(
        _merge_kernel,
        grid=(n // tm,),
        in_specs=[
            tok(D_MODEL), tok(W_BR), tok(W_BR), tok(W_BR),
            pl.BlockSpec((tm, 3 * D_MODEL), lambda t: (t, OFF_MG // (3 * D_MODEL))),
            pl.BlockSpec((None, tm, D_PLE), lambda t: (layer, t, 0)),
            wgt(W_BR, D_MODEL), wgt(W_BR, D_MODEL), wgt(W_BR, D_MODEL),
            wgt(D_MODEL, D_MODEL), wgt(D_MODEL, D_MODEL), wgt(D_PLE, D_MODEL),
            wgt(1, D_MODEL),
        ],
        out_specs=tok(D_MODEL),
        out_shape=jax.ShapeDtypeStruct((n, D_MODEL), F32),
        compiler_params=pltpu.CompilerParams(dimension_semantics=("arbitrary",),
                                             vmem_limit_bytes=VMEM_LIMIT),
    )(x, ya, yb, yc, u, p, wa, wb, wc, wo, wpg, wpe, post_g)


def _t5_bucket(rel):
    nb = T5_BUCKETS // 2
    ret = jnp.where(rel > 0, nb, 0)
    n = jnp.abs(rel)
    max_exact = nb // 2
    nf = jnp.maximum(n, 1).astype(F32)
    large = max_exact + (jnp.log(nf / max_exact) / math.log(T5_MAX_DIST / max_exact)
                         * (nb - max_exact)).astype(jnp.int32)
    large = jnp.minimum(large, nb - 1)
    return ret + jnp.where(n < max_exact, n, large)


def _t5_tiles(t5_table, seq, tq):
    assert tq >= T5_MAX_DIST
    rel = jnp.arange(-(seq - 1), seq)
    rel_bias = t5_table[_t5_bucket(rel)].astype(F32)
    d = jnp.arange(-2, 3)[:, None, None] * tq
    idx = d + jnp.arange(tq)[None, None, :] - jnp.arange(tq)[None, :, None]
    idx = jnp.clip(idx, -(seq - 1), seq - 1) + seq - 1
    return jnp.transpose(rel_bias[idx], (0, 3, 1, 2))


def _rope_tables(seq):
    t = jnp.arange(seq)
    row = (t // GRID_W).astype(F32)
    col = (t % GRID_W).astype(F32)
    n_freq = DH // 4
    inv = ROPE_THETA ** (-jnp.arange(n_freq, dtype=F32) / n_freq)
    ang = jnp.concatenate([row[:, None] * inv] * 2 + [col[:, None] * inv] * 2, axis=1)
    cos, sin = jnp.cos(ang), jnp.sin(ang)
    first_half = (jnp.arange(DH) % (2 * n_freq)) < n_freq
    sin_lo = jnp.where(first_half, -sin, 0.0)
    sin_hi = jnp.where(first_half, 0.0, sin)
    two = lambda a: jnp.concatenate([a, a], axis=1)
    return two(cos), two(sin_lo), two(sin_hi)


def _na_tables(rpb, seq):
    rows = seq // GRID_W
    nblk = rows // 2
    assert rows >= NA_BAND and nblk >= 5
    wr = min(NA_ROWS, rows)
    bis = jnp.array([0, 1, 2, nblk - 2, nblk - 1])[:, None, None]
    t = jnp.arange(NA_QBLK)[None, :, None]
    j = jnp.arange(NA_BAND * GRID_W)[None, None, :]
    qr = 2 * bis + t // GRID_W
    qc = t % GRID_W
    b0 = jnp.clip(2 * bis - NA_ROWS // 2, 0, rows - NA_BAND)
    kr = b0 + j // GRID_W
    kc = j % GRID_W
    sr = jnp.clip(qr - wr // 2, 0, rows - wr)
    sc = jnp.clip(qc - NA_COLS // 2, 0, GRID_W - NA_COLS)
    mask = (kr >= sr) & (kr < sr + wr) & (kc >= sc) & (kc < sc + NA_COLS)
    ri = jnp.clip(kr - qr + NA_ROWS - 1, 0, 2 * NA_ROWS - 2)
    ci = jnp.clip(kc - qc + NA_COLS - 1, 0, 2 * NA_COLS - 2)
    bias = rpb[:, :, ri, ci].astype(F32)
    bias = jnp.where(mask[None, None], bias, -jnp.inf)
    return jnp.transpose(bias, (0, 2, 1, 3, 4))


def _prepare(t5_table, pre_norm_g, w_in, subln_g, q_norm_g, k_norm_g, na_rpb, w_branch_a, w_branch_b,
             w_branch_c, w_out, post_norm_g, w_ple_proj, w_ple_gate, seq, tq):
    depth = w_in.shape[0]
    row_perm = (_b_head_perm()[:, None] * DH + np.arange(DH)[None, :]).reshape(-1)
    two = lambda a: jnp.concatenate([a, a], axis=-1).reshape(depth, 1, LANES)
    return dict(
        w_in=w_in[:, :, _in_proj_column_perm()].astype(BF16),
        pre_g=pre_norm_g.reshape(depth, 1, D_MODEL),
        qn=two(q_norm_g), kn=two(k_norm_g),
        subln=subln_g.reshape(depth, 1, LANES),
        wa=w_branch_a.astype(BF16), wb=w_branch_b[:, row_perm, :].astype(BF16), wc=w_branch_c.astype(BF16),
        wo=w_out.astype(BF16), wpg=w_ple_gate.astype(BF16), wpe=w_ple_proj.astype(BF16),
        post_g=post_norm_g.reshape(depth, 1, D_MODEL),
        t5=_t5_tiles(t5_table, seq, tq), rope=_rope_tables(seq), na=_na_tables(na_rpb, seq),
    )


def _trunk(x, p, lambda_qk, w, *, tm, tq, tk_b, tq_c):
    b, seq, _ = x.shape
    depth = p.shape[0]
    n = b * seq
    xf = x.reshape(n, D_MODEL)
    pf = p.reshape(depth, n, D_PLE)
    for layer in range(depth):
        u = _in_proj(xf, w["pre_g"], w["w_in"], w["qn"], w["kn"], w["rope"], layer, seq, tm)
        u3 = u.reshape(b, seq, IN_W)
        ya = _diff_attn(u3, lambda_qk, w["subln"], w["t5"], layer, tq)
        yb = _gqa_attn(u3, tq, tk_b)
        yc = _na_attn(u3, w["na"], layer, tq_c)
        xf = _merge(xf, ya.reshape(n, W_BR), yb.reshape(n, W_BR), yc.reshape(n, W_BR), u, pf,
                    w["wa"], w["wb"], w["wc"], w["wo"], w["wpg"], w["wpe"], w["post_g"], layer, tm)
    return xf.reshape(b, seq, D_MODEL)


TILES = dict(tm=512, tq=256, tk_b=256, tq_c=512)


def kernel(x_prompt, x_sample, p_prompt, p_sample, t5_table, pre_norm_g, w_in, lambda_qk, subln_g, q_norm_g,
           k_norm_g, na_rpb, w_branch_a, w_branch_b, w_branch_c, w_out, post_norm_g, w_ple_proj, w_ple_gate):
    assert x_prompt.shape[1] == x_sample.shape[1]
    w = _prepare(t5_table, pre_norm_g, w_in, subln_g, q_norm_g, k_norm_g, na_rpb, w_branch_a, w_branch_b,
                 w_branch_c, w_out, post_norm_g, w_ple_proj, w_ple_gate, x_prompt.shape[1], TILES["tq"])
    y_prompt = _trunk(x_prompt, p_prompt, lambda_qk, w, **TILES)
    y_sample = _trunk(x_sample, p_sample, lambda_qk, w, **TILES)
    return (y_prompt, y_sample)
```

```python
import functools
import math

import jax
import jax.numpy as jnp
from jax import lax
from jax.experimental import pallas as pl
from jax.experimental.pallas import tpu as pltpu

F32 = jnp.float32
BF16 = jnp.bfloat16

D_MODEL = 1024
D_PLE = 256
EPS = 1e-6
GRID_W = 64
LANES = 128
DH = 64
H_A = 4
H_B = 8
KV_B = 2
REP_B = H_B // KV_B
H_C = 8
W_BR = 512
NA_ROWS = 8
NA_COLS = 16
NA_BAND = 10
NA_QBLK = 2 * GRID_W
T5_BUCKETS = 32
T5_MAX_DIST = 128
ROPE_THETA = 10000.0
QK_SCALE = DH ** -0.5

OFF_MG = 0
OFF_AQ, OFF_AK, OFF_AV, OFF_AG = 3072, 3584, 4096, 4608
OFF_BQ, OFF_BG = 5120, 5632
OFF_CQ, OFF_CK, OFF_CV, OFF_CG = 6144, 6656, 7168, 7680
OFF_BK, OFF_BV = 8192, 8320
IN_W = 8448

VMEM_LIMIT = 56 * 1024 * 1024


def _pair_major_heads(w, axis):
    shp = w.shape
    w = w.reshape(shp[:axis] + (KV_B, REP_B, DH) + shp[axis + 1:])
    return jnp.swapaxes(w, axis, axis + 1).reshape(shp)


def _permute_in_proj(w_in):
    sizes = dict(aq=512, ak=512, av=512, ag=512, bq=512, bk=128, bv=128, bg=512, cq=512, ck=512, cv=512,
                 cg=512, mg=3072)
    seg, off = {}, 0
    for name, size in sizes.items():
        seg[name] = w_in[:, :, off:off + size]
        off += size
    seg["bq"] = _pair_major_heads(seg["bq"], 2)
    seg["bg"] = _pair_major_heads(seg["bg"], 2)
    order = ("mg", "aq", "ak", "av", "ag", "bq", "bg", "cq", "ck", "cv", "cg", "bk", "bv")
    return jnp.concatenate([seg[n] for n in order], axis=2)


def _silu(x):
    return x * jax.nn.sigmoid(x)


def _dot_nt(a, b):
    return lax.dot_general(a, b, (((1,), (1,)), ((), ())), preferred_element_type=F32)


def _dot(a, b):
    return jnp.dot(a, b, preferred_element_type=F32)


def _pair_norm_rope(a, gain, cos, sin_lo, sin_hi, left):
    sq = a * a
    s_l = jnp.sum(jnp.where(left, sq, 0.0), axis=-1, keepdims=True)
    s_r = jnp.sum(jnp.where(left, 0.0, sq), axis=-1, keepdims=True)
    ms = jnp.where(left, s_l, s_r) * (1.0 / DH)
    y = a * lax.rsqrt(ms + EPS) * gain
    return y * cos + pltpu.roll(y, 112, 1) * sin_lo + pltpu.roll(y, 16, 1) * sin_hi


def _in_proj_kernel(x_ref, g_ref, w_ref, qn_ref, kn_ref, cos_ref, slo_ref, shi_ref, u_ref):
    x = x_ref[...]
    ms = jnp.mean(x * x, axis=-1, keepdims=True)
    h = (x * lax.rsqrt(ms + EPS) * g_ref[...]).astype(BF16)
    tm = x.shape[0]
    left = lax.broadcasted_iota(jnp.int32, (tm, LANES), 1) < DH
    cos, slo, shi = cos_ref[...], slo_ref[...], shi_ref[...]
    for c0 in range(0, OFF_BK, 512):
        acc = _dot(h, w_ref[:, c0:c0 + 512])
        if c0 == OFF_BQ:
            for b in range(4):
                blk = _pair_norm_rope(acc[:, b * LANES:(b + 1) * LANES], qn_ref[...], cos, slo, shi, left)
                u_ref[:, c0 + b * LANES:c0 + (b + 1) * LANES] = (blk * QK_SCALE).astype(BF16)
        else:
            u_ref[:, c0:c0 + 512] = acc.astype(BF16)
    acc = _dot(h, w_ref[:, OFF_BK:IN_W])
    bk = _pair_norm_rope(acc[:, :LANES], kn_ref[...], cos, slo, shi, left)
    u_ref[:, OFF_BK:OFF_BV] = bk.astype(BF16)
    u_ref[:, OFF_BV:IN_W] = acc[:, LANES:].astype(BF16)


def _in_proj(x, pre_g, w_in, qn, kn, rope, layer, seq, tm):
    n = x.shape[0]
    nseq = seq // tm
    cos, slo, shi = rope
    vec = lambda width: pl.BlockSpec((None, 1, width), lambda t: (layer, 0, 0))
    tab = pl.BlockSpec((tm, LANES), lambda t: (t % nseq, 0))
    return pl.pallas_call(
        _in_proj_kernel,
        grid=(n // tm,),
        in_specs=[
            pl.BlockSpec((tm, D_MODEL), lambda t: (t, 0)),
            vec(D_MODEL),
            pl.BlockSpec((None, D_MODEL, IN_W), lambda t: (layer, 0, 0), pipeline_mode=pl.Buffered(1)),
            vec(LANES), vec(LANES), tab, tab, tab,
        ],
        out_specs=pl.BlockSpec((tm, IN_W), lambda t: (t, 0)),
        out_shape=jax.ShapeDtypeStruct((n, IN_W), BF16),
        compiler_params=pltpu.CompilerParams(dimension_semantics=("arbitrary",),
                                             vmem_limit_bytes=VMEM_LIMIT),
        name="in_proj",
    )(x, pre_g, w_in, qn, kn, cos, slo, shi)


def _softmax_pv(s, v):
    m = jnp.max(s, axis=-1, keepdims=True)
    p = jnp.exp(s - m)
    l = jnp.sum(p, axis=-1, keepdims=True)
    return _dot(p.astype(BF16), v) / l


def _lane_halves(q):
    left = lax.broadcasted_iota(jnp.int32, q.shape, 1) < DH
    zero = jnp.zeros_like(q)
    return jnp.where(left, q, zero), jnp.where(left, zero, q)


def _diff_attn_kernel(lam_ref, sg_ref, q_ref, k_ref, v_ref, g_ref, t_ref, o_ref, *, lambda_init, tk):
    i = pl.program_id(2)
    seq = k_ref.shape[0]
    lp = lam_ref[...]
    lam = (jnp.exp(jnp.sum(lp[0:1] * lp[1:2], keepdims=True))
           - jnp.exp(jnp.sum(lp[2:3] * lp[3:4], keepdims=True)) + lambda_init)
    v = v_ref[...]
    outs = []
    for mp, qm in enumerate(_lane_halves(q_ref[...] * QK_SCALE)):
        chunks = []
        for j in range(seq // tk):
            d = jnp.clip(j - i, -2, 2) + 2
            chunks.append(_dot_nt(qm, k_ref[j * tk:(j + 1) * tk, :]) + t_ref[d, mp])
        outs.append(_softmax_pv(jnp.concatenate(chunks, axis=1), v))
    o = outs[0] - lam * outs[1]
    o = o * lax.rsqrt(jnp.mean(o * o, axis=-1, keepdims=True) + EPS) * sg_ref[...]
    o = o * (1.0 - lambda_init)
    o_ref[...] = (o * _silu(g_ref[...].astype(F32))).astype(BF16)


def _diff_attn(u, lam_qk, subln_g, t5_tiles, layer, tq):
    b, seq, _ = u.shape
    lambda_init = 0.8 - 0.6 * math.exp(-0.3 * layer)
    row_blk = lambda off: pl.BlockSpec((None, tq, LANES), lambda bi, h, i: (bi, i, off // LANES + h))
    seq_blk = lambda off: pl.BlockSpec((None, seq, LANES), lambda bi, h, i: (bi, 0, off // LANES + h))
    return pl.pallas_call(
        functools.partial(_diff_attn_kernel, lambda_init=lambda_init, tk=tq),
        grid=(b, H_A, seq // tq),
        in_specs=[
            pl.BlockSpec((None, 4, DH), lambda bi, h, i: (layer, 0, 0)),
            pl.BlockSpec((None, 1, LANES), lambda bi, h, i: (layer, 0, 0)),
            row_blk(OFF_AQ), seq_blk(OFF_AK), seq_blk(OFF_AV), row_blk(OFF_AG),
            pl.BlockSpec((5, 2, tq, tq), lambda bi, h, i: (0, h, 0, 0)),
        ],
        out_specs=pl.BlockSpec((None, tq, LANES), lambda bi, h, i: (bi, i, h)),
        out_shape=jax.ShapeDtypeStruct((b, seq, W_BR), BF16),
        compiler_params=pltpu.CompilerParams(dimension_semantics=("arbitrary",) * 3,
                                             vmem_limit_bytes=VMEM_LIMIT),
        name="diff_attn",
    )(lam_qk, subln_g, u, u, u, u, t5_tiles)


def _gqa_kernel(q_ref, k_ref, v_ref, g_ref, o_ref):
    k = k_ref[...]
    v = v_ref[...]
    o0, o1 = (_softmax_pv(_dot_nt(qm, k), v) for qm in _lane_halves(q_ref[...]))
    left = lax.broadcasted_iota(jnp.int32, o0.shape, 1) < DH
    o = jnp.where(left, o0, o1)
    o_ref[...] = (o * _silu(g_ref[...].astype(F32))).astype(BF16)


def _gqa_attn(u, tq):
    b, seq, _ = u.shape
    row_blk = lambda off: pl.BlockSpec((None, tq, LANES), lambda bi, pr, i: (bi, i, off // LANES + pr))
    seq_blk = lambda off: pl.BlockSpec((None, seq, LANES), lambda bi, pr, i: (bi, 0, off // LANES))
    return pl.pallas_call(
        _gqa_kernel,
        grid=(b, REP_B, seq // tq),
        in_specs=[row_blk(OFF_BQ), seq_blk(OFF_BK), seq_blk(OFF_BV), row_blk(OFF_BG)],
        out_specs=pl.BlockSpec((None, tq, LANES), lambda bi, pr, i: (bi, i, pr)),
        out_shape=jax.ShapeDtypeStruct((b, seq, W_BR), BF16),
        compiler_params=pltpu.CompilerParams(dimension_semantics=("arbitrary",) * 3,
                                             vmem_limit_bytes=VMEM_LIMIT),
        name="gqa_attn",
    )(u, u, u, u)


def _na_block_type(bi, nblk):
    return jnp.where(bi < 2, bi, jnp.where(bi <= nblk - 3, 2, bi - (nblk - 5)))


def _na_kernel(q_ref, k_ref, v_ref, g_ref, tab_ref, o_ref, *, rows):
    big = pl.program_id(1)
    nsub = q_ref.shape[0] // NA_QBLK
    nblk = rows // 2
    nkeys = NA_BAND * GRID_W
    left = lax.broadcasted_iota(jnp.int32, (NA_QBLK, LANES), 1) < DH
    for sub in range(nsub):
        bi = big * nsub + sub
        b0 = jnp.clip(2 * bi - NA_ROWS // 2, 0, rows - NA_BAND)
        typ = _na_block_type(bi, nblk)
        ks = pl.ds(pl.multiple_of(b0 * GRID_W, GRID_W), nkeys)
        rs = slice(sub * NA_QBLK, (sub + 1) * NA_QBLK)
        for pair in range(H_C // 2):
            cs = slice(pair * LANES, (pair + 1) * LANES)
            kb = k_ref[ks, cs]
            vb = v_ref[ks, cs]
            qp = q_ref[rs, cs] * QK_SCALE
            zero = jnp.zeros_like(qp)
            outs = []
            for e in range(2):
                qm = jnp.where(left, qp, zero) if e == 0 else jnp.where(left, zero, qp)
                s = _dot_nt(qm, kb) + tab_ref[typ, 2 * pair + e]
                m = jnp.max(s, axis=-1, keepdims=True)
                p = jnp.exp(s - m)
                l = jnp.sum(p, axis=-1, keepdims=True)
                outs.append(_dot(p.astype(BF16), vb) / l)
            o = jnp.where(left, outs[0], outs[1])
            o_ref[rs, cs] = (o * _silu(g_ref[rs, cs].astype(F32))).astype(BF16)


def _na_attn(u, na_tab, layer, tq):
    b, seq, _ = u.shape
    rows = seq // GRID_W
    row_blk = lambda off: pl.BlockSpec((None, tq, W_BR), lambda bi, i: (bi, i, off // W_BR))
    seq_blk = lambda off: pl.BlockSpec((None, seq, W_BR), lambda bi, i: (bi, 0, off // W_BR))
    return pl.pallas_call(
        functools.partial(_na_kernel, rows=rows),
        grid=(b, seq // tq),
        in_specs=[
            row_blk(OFF_CQ), seq_blk(OFF_CK), seq_blk(OFF_CV), row_blk(OFF_CG),
            pl.BlockSpec((None,) + na_tab.shape[1:], lambda bi, i: (layer, 0, 0, 0, 0),
                         pipeline_mode=pl.Buffered(1)),
        ],
        out_specs=pl.BlockSpec((None, tq, W_BR), lambda bi, i: (bi, i, 0)),
        out_shape=jax.ShapeDtypeStruct((b, seq, W_BR), BF16),
        compiler_params=pltpu.CompilerParams(dimension_semantics=("arbitrary", "arbitrary"),
                                             vmem_limit_bytes=VMEM_LIMIT),
        name="nbr_attn",
    )(u, u, u, u, na_tab)


def _merge_kernel(x_ref, ya_ref, yb_ref, yc_ref, mg_ref, p_ref, wa_ref, wb_ref, wc_ref, wo_ref,
                  wpg_ref, wpe_ref, pg_ref, o_ref):
    def gate(k):
        return jax.nn.sigmoid(mg_ref[:, k * D_MODEL:(k + 1) * D_MODEL].astype(F32))

    m = gate(0) * _dot(ya_ref[...], wa_ref[...])
    m = m + gate(1) * _dot(yb_ref[...], wb_ref[...])
    m = m + gate(2) * _dot(yc_ref[...], wc_ref[...])
    r = _dot(m.astype(BF16), wo_ref[...])
    r = r * lax.rsqrt(jnp.mean(r * r, axis=-1, keepdims=True) + EPS) * pg_ref[...]
    x = x_ref[...] + r
    emb = _dot(p_ref[...].astype(BF16), wpe_ref[...])
    o_ref[...] = x + jax.nn.sigmoid(_dot(x.astype(BF16), wpg_ref[...])) * emb


def _merge(x, ya, yb, yc, u, p, wa, wb, wc, wo, wpg, wpe, post_g, layer, tm):
    n = x.shape[0]
    tok = lambda width: pl.BlockSpec((tm, width), lambda t: (t, 0))
    wgt = lambda r, c: pl.BlockSpec((None, r, c), lambda t: (layer, 0, 0))
    return pl.pallas_call(
        _merge_kernel,
        grid=(n // tm,),
        in_specs=[
            tok(D_MODEL), tok(W_BR), tok(W_BR), tok(W_BR),
            pl.BlockSpec((tm, 3 * D_MODEL), lambda t: (t, OFF_MG // (3 * D_MODEL))),
            pl.BlockSpec((None, tm, D_PLE), lambda t: (layer, t, 0)),
            wgt(W_BR, D_MODEL), wgt(W_BR, D_MODEL), wgt(W_BR, D_MODEL),
            wgt(D_MODEL, D_MODEL), wgt(D_MODEL, D_MODEL), wgt(D_PLE, D_MODEL),
            wgt(1, D_MODEL),
        ],
        out_specs=tok(D_MODEL),
        out_shape=jax.ShapeDtypeStruct((n, D_MODEL), F32),
        compiler_params=pltpu.CompilerParams(dimension_semantics=("arbitrary",),
                                             vmem_limit_bytes=VMEM_LIMIT),
        name="merge",
    )(x, ya, yb, yc, u, p, wa, wb, wc, wo, wpg, wpe, post_g)


def _t5_bucket(rel):
    nb = T5_BUCKETS // 2
    ret = jnp.where(rel > 0, nb, 0)
    n = jnp.abs(rel)
    max_exact = nb // 2
    nf = jnp.maximum(n, 1).astype(F32)
    large = max_exact + (jnp.log(nf / max_exact) / math.log(T5_MAX_DIST / max_exact)
                         * (nb - max_exact)).astype(jnp.int32)
    large = jnp.minimum(large, nb - 1)
    return ret + jnp.where(n < max_exact, n, large)


def _t5_tiles(t5_table, seq, tq):
    assert tq >= T5_MAX_DIST
    rel = jnp.arange(-(seq - 1), seq)
    rel_bias = t5_table[_t5_bucket(rel)].astype(F32)
    d = jnp.arange(-2, 3)[:, None, None] * tq
    idx = d + jnp.arange(tq)[None, None, :] - jnp.arange(tq)[None, :, None]
    idx = jnp.clip(idx, -(seq - 1), seq - 1) + seq - 1
    return jnp.transpose(rel_bias[idx], (0, 3, 1, 2))


def _rope_tables(seq):
    t = jnp.arange(seq)
    row = (t // GRID_W).astype(F32)
    col = (t % GRID_W).astype(F32)
    n_freq = DH // 4
    inv = ROPE_THETA ** (-jnp.arange(n_freq, dtype=F32) / n_freq)
    ang = jnp.concatenate([row[:, None] * inv] * 2 + [col[:, None] * inv] * 2, axis=1)
    cos, sin = jnp.cos(ang), jnp.sin(ang)
    first_half = (jnp.arange(DH) % (2 * n_freq)) < n_freq
    sin_lo = jnp.where(first_half, -sin, 0.0)
    sin_hi = jnp.where(first_half, 0.0, sin)
    two = lambda a: jnp.concatenate([a, a], axis=1)
    return two(cos), two(sin_lo), two(sin_hi)


def _na_tables(rpb, seq):
    rows = seq // GRID_W
    nblk = rows // 2
    assert rows >= NA_BAND and nblk >= 5
    wr = min(NA_ROWS, rows)
    depth = rpb.shape[0]
    ncol = 2 * NA_COLS - 1
    bis = jnp.array([0, 1, 2, nblk - 2, nblk - 1])[:, None, None, None, None]
    qr = 2 * bis + jnp.arange(2)[None, :, None, None, None]
    qc = jnp.arange(GRID_W)[None, None, :, None, None]
    kr = (jnp.clip(2 * bis - NA_ROWS // 2, 0, rows - NA_BAND)
          + jnp.arange(NA_BAND)[None, None, None, :, None])
    kc = jnp.arange(GRID_W)[None, None, None, None, :]
    sr = jnp.clip(qr - wr // 2, 0, rows - wr)
    sc = jnp.clip(qc - NA_COLS // 2, 0, GRID_W - NA_COLS)
    mask = (kr >= sr) & (kr < sr + wr) & (kc >= sc) & (kc < sc + NA_COLS)
    ri = jnp.clip(kr - qr + NA_ROWS - 1, 0, 2 * NA_ROWS - 2)[:, :, 0, :, 0]
    ci = jnp.clip(kc - qc + NA_COLS - 1, 0, ncol - 1)[0, 0, :, 0, :]
    by_row = rpb.astype(F32)[:, :, ri, :]
    onehot = (jnp.arange(ncol)[:, None] == ci.reshape(1, -1)).astype(F32)
    bias = jnp.dot(by_row.reshape(-1, ncol), onehot, precision=lax.Precision.HIGHEST)
    bias = bias.reshape(depth, H_C, 5, 2, NA_BAND, GRID_W, GRID_W)
    bias = jnp.transpose(bias, (0, 2, 1, 3, 5, 4, 6))
    bias = jnp.where(mask[None, :, None], bias, -jnp.inf)
    return bias.reshape(depth, 5, H_C, NA_QBLK, NA_BAND * GRID_W)


def _prepare(t5_table, pre_norm_g, w_in, subln_g, q_norm_g, k_norm_g, na_rpb, w_branch_a, w_branch_b,
             w_branch_c, w_out, post_norm_g, w_ple_proj, w_ple_gate, seq, tq):
    depth = w_in.shape[0]
    two = lambda a: jnp.concatenate([a, a], axis=-1).reshape(depth, 1, LANES)
    return dict(
        w_in=_permute_in_proj(w_in).astype(BF16),
        pre_g=pre_norm_g.reshape(depth, 1, D_MODEL),
        qn=two(q_norm_g), kn=two(k_norm_g),
        subln=subln_g.reshape(depth, 1, LANES),
        wa=w_branch_a.astype(BF16), wb=_pair_major_heads(w_branch_b, 1).astype(BF16), wc=w_branch_c.astype(BF16),
        wo=w_out.astype(BF16), wpg=w_ple_gate.astype(BF16), wpe=w_ple_proj.astype(BF16),
        post_g=post_norm_g.reshape(depth, 1, D_MODEL),
        t5=_t5_tiles(t5_table, seq, tq), rope=_rope_tables(seq), na=_na_tables(na_rpb, seq),
    )


def _trunk(x, p, lambda_qk, w, *, tm, tq, tq_c):
    b, seq, _ = x.shape
    depth = p.shape[0]
    n = b * seq
    xf = x.reshape(n, D_MODEL)
    pf = p.reshape(depth, n, D_PLE)
    for layer in range(depth):
        u = _in_proj(xf, w["pre_g"], w["w_in"], w["qn"], w["kn"], w["rope"], layer, seq, tm)
        u3 = u.reshape(b, seq, IN_W)
        ya = _diff_attn(u3, lambda_qk, w["subln"], w["t5"], layer, tq)
        yb = _gqa_attn(u3, tq)
        yc = _na_attn(u3, w["na"], layer, tq_c)
        xf = _merge(xf, ya.reshape(n, W_BR), yb.reshape(n, W_BR), yc.reshape(n, W_BR), u, pf,
                    w["wa"], w["wb"], w["wc"], w["wo"], w["wpg"], w["wpe"], w["post_g"], layer, tm)
    return xf.reshape(b, seq, D_MODEL)


TILES = dict(tm=512, tq=256, tq_c=512)


def kernel(x_prompt, x_sample, p_prompt, p_sample, t5_table, pre_norm_g, w_in, lambda_qk, subln_g, q_norm_g,
           k_norm_g, na_rpb, w_branch_a, w_branch_b, w_branch_c, w_out, post_norm_g, w_ple_proj, w_ple_gate):
    assert x_prompt.shape[1] == x_sample.shape[1]
    w = _prepare(t5_table, pre_norm_g, w_in, subln_g, q_norm_g, k_norm_g, na_rpb, w_branch_a, w_branch_b,
                 w_branch_c, w_out, post_norm_g, w_ple_proj, w_ple_gate, x_prompt.shape[1], TILES["tq"])
    y_prompt = _trunk(x_prompt, p_prompt, lambda_qk, w, **TILES)
    y_sample = _trunk(x_sample, p_sample, lambda_qk, w, **TILES)
    return (y_prompt, y_sample)
```

```python
import functools
import math

import jax
import jax.numpy as jnp
from jax import lax
from jax.experimental import pallas as pl
from jax.experimental.pallas import tpu as pltpu

F32 = jnp.float32
BF16 = jnp.bfloat16

D_MODEL = 1024
D_PLE = 256
EPS = 1e-6
GRID_W = 64
LANES = 128
DH = 64
H_A = 4
H_B = 8
KV_B = 2
REP_B = H_B // KV_B
H_C = 8
W_BR = 512
NA_ROWS = 8
NA_COLS = 16
NA_BAND = 10
NA_QBLK = 2 * GRID_W
T5_BUCKETS = 32
T5_MAX_DIST = 128
ATTN_TK = 256
T5_TK = ATTN_TK
ROPE_THETA = 10000.0
QK_SCALE = DH ** -0.5

OFF_MG = 0
OFF_AQ, OFF_AK, OFF_AV, OFF_AG = 3072, 3584, 4096, 4608
OFF_BQ, OFF_BG = 5120, 5632
OFF_CQ, OFF_CK, OFF_CV, OFF_CG = 6144, 6656, 7168, 7680
OFF_BK, OFF_BV = 8192, 8320
IN_W = 8448

VMEM_LIMIT = 56 * 1024 * 1024


def _pair_major_heads(w, axis):
    shp = w.shape
    w = w.reshape(shp[:axis] + (KV_B, REP_B, DH) + shp[axis + 1:])
    return jnp.swapaxes(w, axis, axis + 1).reshape(shp)


def _permute_in_proj(w_in):
    sizes = dict(aq=512, ak=512, av=512, ag=512, bq=512, bk=128, bv=128, bg=512, cq=512, ck=512, cv=512,
                 cg=512, mg=3072)
    seg, off = {}, 0
    for name, size in sizes.items():
        seg[name] = w_in[:, :, off:off + size]
        off += size
    seg["bq"] = _pair_major_heads(seg["bq"], 2)
    seg["bg"] = _pair_major_heads(seg["bg"], 2)
    order = ("mg", "aq", "ak", "av", "ag", "bq", "bg", "cq", "ck", "cv", "cg", "bk", "bv")
    return jnp.concatenate([seg[n] for n in order], axis=2)


def _silu(x):
    return x * jax.nn.sigmoid(x)


def _dot_nt(a, b):
    return lax.dot_general(a, b, (((1,), (1,)), ((), ())), preferred_element_type=F32)


def _dot(a, b):
    return jnp.dot(a, b, preferred_element_type=F32)


def _pair_norm_rope(a, gain, cos, sin_lo, sin_hi, left):
    sq = a * a
    s_l = jnp.sum(jnp.where(left, sq, 0.0), axis=-1, keepdims=True)
    s_r = jnp.sum(jnp.where(left, 0.0, sq), axis=-1, keepdims=True)
    ms = jnp.where(left, s_l, s_r) * (1.0 / DH)
    y = a * lax.rsqrt(ms + EPS) * gain
    return y * cos + pltpu.roll(y, 112, 1) * sin_lo + pltpu.roll(y, 16, 1) * sin_hi


def _in_proj_kernel(x_ref, g_ref, w_ref, qn_ref, kn_ref, cos_ref, slo_ref, shi_ref, u_ref):
    x = x_ref[...]
    ms = jnp.mean(x * x, axis=-1, keepdims=True)
    h = (x * lax.rsqrt(ms + EPS) * g_ref[...]).astype(BF16)
    tm = x.shape[0]
    left = lax.broadcasted_iota(jnp.int32, (tm, LANES), 1) < DH
    cos, slo, shi = cos_ref[...], slo_ref[...], shi_ref[...]
    for c0 in range(0, OFF_BK, 512):
        acc = _dot(h, w_ref[:, c0:c0 + 512])
        if c0 == OFF_BQ:
            for b in range(4):
                blk = _pair_norm_rope(acc[:, b * LANES:(b + 1) * LANES], qn_ref[...], cos, slo, shi, left)
                u_ref[:, c0 + b * LANES:c0 + (b + 1) * LANES] = (blk * QK_SCALE).astype(BF16)
        else:
            u_ref[:, c0:c0 + 512] = acc.astype(BF16)
    acc = _dot(h, w_ref[:, OFF_BK:IN_W])
    bk = _pair_norm_rope(acc[:, :LANES], kn_ref[...], cos, slo, shi, left)
    u_ref[:, OFF_BK:OFF_BV] = bk.astype(BF16)
    u_ref[:, OFF_BV:IN_W] = acc[:, LANES:].astype(BF16)


def _in_proj(x, pre_g, w_in, qn, kn, rope, layer, seq, tm):
    n = x.shape[0]
    nseq = seq // tm
    cos, slo, shi = rope
    vec = lambda width: pl.BlockSpec((None, 1, width), lambda t: (layer, 0, 0))
    tab = pl.BlockSpec((tm, LANES), lambda t: (t % nseq, 0))
    return pl.pallas_call(
        _in_proj_kernel,
        grid=(n // tm,),
        in_specs=[
            pl.BlockSpec((tm, D_MODEL), lambda t: (t, 0)),
            vec(D_MODEL),
            pl.BlockSpec((None, D_MODEL, IN_W), lambda t: (layer, 0, 0), pipeline_mode=pl.Buffered(1)),
            vec(LANES), vec(LANES), tab, tab, tab,
        ],
        out_specs=pl.BlockSpec((tm, IN_W), lambda t: (t, 0)),
        out_shape=jax.ShapeDtypeStruct((n, IN_W), BF16),
        compiler_params=pltpu.CompilerParams(dimension_semantics=("arbitrary",),
                                             vmem_limit_bytes=VMEM_LIMIT),
        name="in_proj",
    )(x, pre_g, w_in, qn, kn, cos, slo, shi)


def _fold_lanes(x, op):
    out = x[:, :LANES]
    for c in range(1, x.shape[1] // LANES):
        out = op(out, x[:, c * LANES:(c + 1) * LANES])
    return out


def _two_stage_scratch(tq, seq):
    return [pltpu.VMEM((2, tq, seq), F32), pltpu.VMEM((2, tq, LANES), F32)] * 2


def _two_stage_step(qs, k_ref, v_ref, bias, s_w, m_w, s_r, m_r, tk):
    tq, seq = s_w.shape[1:]
    m_prev = [jnp.concatenate([m_r[e]] * (tk // LANES), axis=1) for e in range(2)]
    m_run, l_run, acc = [None] * 2, [None] * 2, [None] * 2
    for c in range(seq // tk):
        ks = slice(c * tk, (c + 1) * tk)
        kc, vc = k_ref[ks, :], v_ref[ks, :]
        for e in range(2):
            s = _dot_nt(qs[e], kc)
            if bias is not None:
                s = s + bias(c, e)
            s_w[e, :, ks] = s
            m_c = _fold_lanes(s, jnp.maximum)
            m_run[e] = m_c if c == 0 else jnp.maximum(m_run[e], m_c)
            p = jnp.exp(s_r[e, :, ks] - m_prev[e])
            l_c = _fold_lanes(p, jnp.add)
            l_run[e] = l_c if c == 0 else l_run[e] + l_c
            pv = _dot(p.astype(BF16), vc)
            acc[e] = pv if c == 0 else acc[e] + pv
    for e in range(2):
        m_w[e] = jnp.broadcast_to(jnp.max(m_run[e], axis=-1, keepdims=True), (tq, LANES))
    return [acc[e] / jnp.sum(l_run[e], axis=-1, keepdims=True) for e in range(2)]


def _two_stage(step, scratch):
    s_a, m_a, s_b, m_b = scratch
    t = pl.program_id(0)

    @pl.when(t == 0)
    def _():
        s_b[...] = jnp.zeros_like(s_b)
        m_b[...] = jnp.zeros_like(m_b)

    @pl.when(t % 2 == 0)
    def _():
        step(s_a, m_a, s_b, m_b)

    @pl.when(t % 2 == 1)
    def _():
        step(s_b, m_b, s_a, m_a)


def _lane_halves(q):
    left = lax.broadcasted_iota(jnp.int32, q.shape, 1) < DH
    zero = jnp.zeros_like(q)
    return jnp.where(left, q, zero), jnp.where(left, zero, q)


def _unit_maps(b, heads, n_i):
    units = b * heads * n_i

    def unit(t, lag):
        un = jnp.clip(t - lag, 0, units - 1)
        return un // (heads * n_i), (un // n_i) % heads, un % n_i

    return units, unit


def _diff_attn_kernel(lam_ref, sg_ref, q_ref, k_ref, v_ref, g_ref, t_ref, o_ref, *scratch,
                      lambda_init, tk, n_i, units):
    i = jnp.minimum(pl.program_id(0), units - 1) % n_i
    per_q = q_ref.shape[0] // tk

    def bias(c, mp):
        return t_ref[jnp.clip(c - i * per_q, -2, per_q + 1) + 2, mp]

    def step(*bufs):
        lp = lam_ref[...]
        lam = (jnp.exp(jnp.sum(lp[0:1] * lp[1:2], keepdims=True))
               - jnp.exp(jnp.sum(lp[2:3] * lp[3:4], keepdims=True)) + lambda_init)
        o0, o1 = _two_stage_step(_lane_halves(q_ref[...] * QK_SCALE), k_ref, v_ref, bias, *bufs, tk)
        o = o0 - lam * o1
        o = o * lax.rsqrt(jnp.mean(o * o, axis=-1, keepdims=True) + EPS) * sg_ref[...]
        o = o * (1.0 - lambda_init)
        o_ref[...] = (o * _silu(g_ref[...].astype(F32))).astype(BF16)

    _two_stage(step, scratch)


def _diff_attn(u, lam_qk, subln_g, t5_tiles, layer, tq):
    b, seq, _ = u.shape
    ntile, _, _, tk = t5_tiles.shape
    n_i = seq // tq
    units, unit = _unit_maps(b, H_A, n_i)
    lambda_init = 0.8 - 0.6 * math.exp(-0.3 * layer)

    def row_blk(off, lag):
        def index(t):
            bi, h, i = unit(t, lag)
            return bi, i, off // LANES + h
        return pl.BlockSpec((None, tq, LANES), index)

    def seq_blk(off, lag):
        def index(t):
            bi, h, _ = unit(t, lag)
            return bi, 0, off // LANES + h
        return pl.BlockSpec((None, seq, LANES), index)

    return pl.pallas_call(
        functools.partial(_diff_attn_kernel, lambda_init=lambda_init, tk=tk, n_i=n_i, units=units),
        grid=(units + 1,),
        in_specs=[
            pl.BlockSpec((None, 4, DH), lambda t: (layer, 0, 0)),
            pl.BlockSpec((None, 1, LANES), lambda t: (layer, 0, 0)),
            row_blk(OFF_AQ, 0), seq_blk(OFF_AK, 0), seq_blk(OFF_AV, 1), row_blk(OFF_AG, 1),
            pl.BlockSpec((ntile, 2, tq, tk), lambda t: (0, unit(t, 0)[1], 0, 0)),
        ],
        out_specs=row_blk(0, 1),
        out_shape=jax.ShapeDtypeStruct((b, seq, W_BR), BF16),
        scratch_shapes=_two_stage_scratch(tq, seq),
        compiler_params=pltpu.CompilerParams(dimension_semantics=("arbitrary",),
                                             vmem_limit_bytes=VMEM_LIMIT),
        name="diff_attn",
    )(lam_qk, subln_g, u, u, u, u, t5_tiles)


def _gqa_kernel(q_ref, k_ref, v_ref, g_ref, o_ref, *scratch, tk):
    def step(*bufs):
        o0, o1 = _two_stage_step(_lane_halves(q_ref[...]), k_ref, v_ref, None, *bufs, tk)
        left = lax.broadcasted_iota(jnp.int32, o0.shape, 1) < DH
        o = jnp.where(left, o0, o1)
        o_ref[...] = (o * _silu(g_ref[...].astype(F32))).astype(BF16)

    _two_stage(step, scratch)


def _gqa_attn(u, tq):
    b, seq, _ = u.shape
    units, unit = _unit_maps(b, REP_B, seq // tq)

    def row_blk(off, lag):
        def index(t):
            bi, pr, i = unit(t, lag)
            return bi, i, off // LANES + pr
        return pl.BlockSpec((None, tq, LANES), index)

    def seq_blk(off, lag):
        return pl.BlockSpec((None, seq, LANES), lambda t: (unit(t, lag)[0], 0, off // LANES))

    return pl.pallas_call(
        functools.partial(_gqa_kernel, tk=ATTN_TK),
        grid=(units + 1,),
        in_specs=[row_blk(OFF_BQ, 0), seq_blk(OFF_BK, 0), seq_blk(OFF_BV, 1), row_blk(OFF_BG, 1)],
        out_specs=row_blk(0, 1),
        out_shape=jax.ShapeDtypeStruct((b, seq, W_BR), BF16),
        scratch_shapes=_two_stage_scratch(tq, seq),
        compiler_params=pltpu.CompilerParams(dimension_semantics=("arbitrary",),
                                             vmem_limit_bytes=VMEM_LIMIT),
        name="gqa_attn",
    )(u, u, u, u)


def _na_block_type(bi, nblk):
    return jnp.where(bi < 2, bi, jnp.where(bi <= nblk - 3, 2, bi - (nblk - 5)))


def _na_kernel(q_ref, k_ref, v_ref, g_ref, tab_ref, o_ref, *, rows):
    big = pl.program_id(1)
    nsub = q_ref.shape[0] // NA_QBLK
    nblk = rows // 2
    nkeys = NA_BAND * GRID_W
    left = lax.broadcasted_iota(jnp.int32, (NA_QBLK, LANES), 1) < DH
    for sub in range(nsub):
        bi = big * nsub + sub
        b0 = jnp.clip(2 * bi - NA_ROWS // 2, 0, rows - NA_BAND)
        typ = _na_block_type(bi, nblk)
        ks = pl.ds(pl.multiple_of(b0 * GRID_W, GRID_W), nkeys)
        rs = slice(sub * NA_QBLK, (sub + 1) * NA_QBLK)
        for pair in range(H_C // 2):
            cs = slice(pair * LANES, (pair + 1) * LANES)
            kb = k_ref[ks, cs]
            vb = v_ref[ks, cs]
            qp = q_ref[rs, cs] * QK_SCALE
            zero = jnp.zeros_like(qp)
            outs = []
            for e in range(2):
                qm = jnp.where(left, qp, zero) if e == 0 else jnp.where(left, zero, qp)
                s = _dot_nt(qm, kb) + tab_ref[typ, 2 * pair + e]
                m = jnp.max(s, axis=-1, keepdims=True)
                p = jnp.exp(s - m)
                l = jnp.sum(p, axis=-1, keepdims=True)
                outs.append(_dot(p.astype(BF16), vb) / l)
            o = jnp.where(left, outs[0], outs[1])
            o_ref[rs, cs] = (o * _silu(g_ref[rs, cs].astype(F32))).astype(BF16)


def _na_attn(u, na_tab, layer, tq):
    b, seq, _ = u.shape
    rows = seq // GRID_W
    row_blk = lambda off: pl.BlockSpec((None, tq, W_BR), lambda bi, i: (bi, i, off // W_BR))
    seq_blk = lambda off: pl.BlockSpec((None, seq, W_BR), lambda bi, i: (bi, 0, off // W_BR))
    return pl.pallas_call(
        functools.partial(_na_kernel, rows=rows),
        grid=(b, seq // tq),
        in_specs=[
            row_blk(OFF_CQ), seq_blk(OFF_CK), seq_blk(OFF_CV), row_blk(OFF_CG),
            pl.BlockSpec((None,) + na_tab.shape[1:], lambda bi, i: (layer, 0, 0, 0, 0),
                         pipeline_mode=pl.Buffered(1)),
        ],
        out_specs=pl.BlockSpec((None, tq, W_BR), lambda bi, i: (bi, i, 0)),
        out_shape=jax.ShapeDtypeStruct((b, seq, W_BR), BF16),
        compiler_params=pltpu.CompilerParams(dimension_semantics=("arbitrary", "arbitrary"),
                                             vmem_limit_bytes=VMEM_LIMIT),
        name="nbr_attn",
    )(u, u, u, u, na_tab)


def _merge_kernel(x_ref, ya_ref, yb_ref, yc_ref, mg_ref, p_ref, wa_ref, wb_ref, wc_ref, wo_ref,
                  wpg_ref, wpe_ref, pg_ref, o_ref):
    def gate(k):
        return jax.nn.sigmoid(mg_ref[:, k * D_MODEL:(k + 1) * D_MODEL].astype(F32))

    m = gate(0) * _dot(ya_ref[...], wa_ref[...])
    m = m + gate(1) * _dot(yb_ref[...], wb_ref[...])
    m = m + gate(2) * _dot(yc_ref[...], wc_ref[...])
    r = _dot(m.astype(BF16), wo_ref[...])
    r = r * lax.rsqrt(jnp.mean(r * r, axis=-1, keepdims=True) + EPS) * pg_ref[...]
    x = x_ref[...] + r
    emb = _dot(p_ref[...].astype(BF16), wpe_ref[...])
    o_ref[...] = x + jax.nn.sigmoid(_dot(x.astype(BF16), wpg_ref[...])) * emb


def _merge(x, ya, yb, yc, u, p, wa, wb, wc, wo, wpg, wpe, post_g, layer, tm):
    n = x.shape[0]
    tok = lambda width: pl.BlockSpec((tm, width), lambda t: (t, 0))
    wgt = lambda r, c: pl.BlockSpec((None, r, c), lambda t: (layer, 0, 0))
    return pl.pallas_call(
        _merge_kernel,
        grid=(n // tm,),
        in_specs=[
            tok(D_MODEL), tok(W_BR), tok(W_BR), tok(W_BR),
            pl.BlockSpec((tm, 3 * D_MODEL), lambda t: (t, OFF_MG // (3 * D_MODEL))),
            pl.BlockSpec((None, tm, D_PLE), lambda t: (layer, t, 0)),
            wgt(W_BR, D_MODEL), wgt(W_BR, D_MODEL), wgt(W_BR, D_MODEL),
            wgt(D_MODEL, D_MODEL), wgt(D_MODEL, D_MODEL), wgt(D_PLE, D_MODEL),
            wgt(1, D_MODEL),
        ],
        out_specs=tok(D_MODEL),
        out_shape=jax.ShapeDtypeStruct((n, D_MODEL), F32),
        compiler_params=pltpu.CompilerParams(dimension_semantics=("arbitrary",),
                                             vmem_limit_bytes=VMEM_LIMIT),
        name="merge",
    )(x, ya, yb, yc, u, p, wa, wb, wc, wo, wpg, wpe, post_g)


def _t5_bucket(rel):
    nb = T5_BUCKETS // 2
    ret = jnp.where(rel > 0, nb, 0)
    n = jnp.abs(rel)
    max_exact = nb // 2
    nf = jnp.maximum(n, 1).astype(F32)
    large = max_exact + (jnp.log(nf / max_exact) / math.log(T5_MAX_DIST / max_exact)
                         * (nb - max_exact)).astype(jnp.int32)
    large = jnp.minimum(large, nb - 1)
    return ret + jnp.where(n < max_exact, n, large)


def _t5_tiles(t5_table, seq, tq, tk):
    assert tk >= T5_MAX_DIST and tq % tk == 0
    rel = jnp.arange(-(seq - 1), seq)
    rel_bias = t5_table[_t5_bucket(rel)].astype(F32).T
    big = tq + (tq // tk + 2) * tk
    padded = jnp.pad(rel_bias, ((0, 0), (big, big)), mode="edge")
    period = tq + tk
    tiles = []
    for e in range(-2, tq // tk + 2):
        base = seq - 1 + big + e * tk
        seg = padded[:, base - tq:base + tk]
        w = jnp.concatenate([seg[:, tq:], seg[:, :tq]], axis=1)
        toe = jnp.tile(w, (1, tq))[:, :tq * (period - 1)].reshape(-1, tq, period - 1)
        tiles.append(toe[:, :, :tk])
    return jnp.stack(tiles)


def _rope_tables(seq):
    t = jnp.arange(seq)
    row = (t // GRID_W).astype(F32)
    col = (t % GRID_W).astype(F32)
    n_freq = DH // 4
    inv = ROPE_THETA ** (-jnp.arange(n_freq, dtype=F32) / n_freq)
    ang = jnp.concatenate([row[:, None] * inv] * 2 + [col[:, None] * inv] * 2, axis=1)
    cos, sin = jnp.cos(ang), jnp.sin(ang)
    first_half = (jnp.arange(DH) % (2 * n_freq)) < n_freq
    sin_lo = jnp.where(first_half, -sin, 0.0)
    sin_hi = jnp.where(first_half, 0.0, sin)
    two = lambda a: jnp.concatenate([a, a], axis=1)
    return two(cos), two(sin_lo), two(sin_hi)


def _na_tables(rpb, seq):
    rows = seq // GRID_W
    nblk = rows // 2
    assert rows >= NA_BAND and nblk >= 5
    wr = min(NA_ROWS, rows)
    depth = rpb.shape[0]
    ncol = 2 * NA_COLS - 1
    bis = jnp.array([0, 1, 2, nblk - 2, nblk - 1])[:, None, None, None, None]
    qr = 2 * bis + jnp.arange(2)[None, :, None, None, None]
    qc = jnp.arange(GRID_W)[None, None, :, None, None]
    kr = (jnp.clip(2 * bis - NA_ROWS // 2, 0, rows - NA_BAND)
          + jnp.arange(NA_BAND)[None, None, None, :, None])
    kc = jnp.arange(GRID_W)[None, None, None, None, :]
    sr = jnp.clip(qr - wr // 2, 0, rows - wr)
    sc = jnp.clip(qc - NA_COLS // 2, 0, GRID_W - NA_COLS)
    mask = (kr >= sr) & (kr < sr + wr) & (kc >= sc) & (kc < sc + NA_COLS)
    ri = jnp.clip(kr - qr + NA_ROWS - 1, 0, 2 * NA_ROWS - 2)[:, :, 0, :, 0]
    ci = jnp.clip(kc - qc + NA_COLS - 1, 0, ncol - 1)[0, 0, :, 0, :]
    by_row = rpb.astype(F32)[:, :, ri, :]
    onehot = (jnp.arange(ncol)[:, None] == ci.reshape(1, -1)).astype(F32)
    bias = jnp.dot(by_row.reshape(-1, ncol), onehot, precision=lax.Precision.HIGHEST)
    bias = bias.reshape(depth, H_C, 5, 2, NA_BAND, GRID_W, GRID_W)
    bias = jnp.transpose(bias, (0, 2, 1, 3, 5, 4, 6))
    bias = jnp.where(mask[None, :, None], bias, -jnp.inf)
    return bias.reshape(depth, 5, H_C, NA_QBLK, NA_BAND * GRID_W)


def _prepare(t5_table, pre_norm_g, w_in, subln_g, q_norm_g, k_norm_g, na_rpb, w_branch_a, w_branch_b,
             w_branch_c, w_out, post_norm_g, w_ple_proj, w_ple_gate, seq, tq):
    depth = w_in.shape[0]
    two = lambda a: jnp.concatenate([a, a], axis=-1).reshape(depth, 1, LANES)
    return dict(
        w_in=_permute_in_proj(w_in).astype(BF16),
        pre_g=pre_norm_g.reshape(depth, 1, D_MODEL),
        qn=two(q_norm_g), kn=two(k_norm_g),
        subln=subln_g.reshape(depth, 1, LANES),
        wa=w_branch_a.astype(BF16), wb=_pair_major_heads(w_branch_b, 1).astype(BF16), wc=w_branch_c.astype(BF16),
        wo=w_out.astype(BF16), wpg=w_ple_gate.astype(BF16), wpe=w_ple_proj.astype(BF16),
        post_g=post_norm_g.reshape(depth, 1, D_MODEL),
        t5=_t5_tiles(t5_table, seq, tq, T5_TK), rope=_rope_tables(seq), na=_na_tables(na_rpb, seq),
    )


def _trunk(x, p, lambda_qk, w, *, tm, tq, tq_c):
    b, seq, _ = x.shape
    depth = p.shape[0]
    n = b * seq
    xf = x.reshape(n, D_MODEL)
    pf = p.reshape(depth, n, D_PLE)
    for layer in range(depth):
        u = _in_proj(xf, w["pre_g"], w["w_in"], w["qn"], w["kn"], w["rope"], layer, seq, tm)
        u3 = u.reshape(b, seq, IN_W)
        ya = _diff_attn(u3, lambda_qk, w["subln"], w["t5"], layer, tq)
        yb = _gqa_attn(u3, tq)
        yc = _na_attn(u3, w["na"], layer, tq_c)
        xf = _merge(xf, ya.reshape(n, W_BR), yb.reshape(n, W_BR), yc.reshape(n, W_BR), u, pf,
                    w["wa"], w["wb"], w["wc"], w["wo"], w["wpg"], w["wpe"], w["post_g"], layer, tm)
    return xf.reshape(b, seq, D_MODEL)


TILES = dict(tm=512, tq=256, tq_c=512)


def kernel(x_prompt, x_sample, p_prompt, p_sample, t5_table, pre_norm_g, w_in, lambda_qk, subln_g, q_norm_g,
           k_norm_g, na_rpb, w_branch_a, w_branch_b, w_branch_c, w_out, post_norm_g, w_ple_proj, w_ple_gate):
    assert x_prompt.shape[1] == x_sample.shape[1]
    w = _prepare(t5_table, pre_norm_g, w_in, subln_g, q_norm_g, k_norm_g, na_rpb, w_branch_a, w_branch_b,
                 w_branch_c, w_out, post_norm_g, w_ple_proj, w_ple_gate, x_prompt.shape[1], TILES["tq"])
    y_prompt = _trunk(x_prompt, p_prompt, lambda_qk, w, **TILES)
    y_sample = _trunk(x_sample, p_sample, lambda_qk, w, **TILES)
    return (y_prompt, y_sample)
```

```python
import functools
import math

import jax
import jax.numpy as jnp
from jax import lax
from jax.experimental import pallas as pl
from jax.experimental.pallas import tpu as pltpu

F32 = jnp.float32
BF16 = jnp.bfloat16

D_MODEL = 1024
D_PLE = 256
EPS = 1e-6
GRID_W = 64
LANES = 128
SUBLANES = 8
ONES_ROWS = 16
DH = 64
H_A = 4
H_B = 8
KV_B = 2
REP_B = H_B // KV_B
H_C = 8
W_BR = 512
NA_ROWS = 8
NA_COLS = 16
NA_BAND = 10
NA_QBLK = 2 * GRID_W
T5_BUCKETS = 32
T5_MAX_DIST = 128
ATTN_TK = 256
T5_TK = ATTN_TK
ROPE_THETA = 10000.0
QK_SCALE = DH ** -0.5

OFF_MG = 0
OFF_AQ, OFF_AK, OFF_AV, OFF_AG = 3072, 3584, 4096, 4608
OFF_BQ, OFF_BG = 5120, 5632
OFF_CQ, OFF_CK, OFF_CV, OFF_CG = 6144, 6656, 7168, 7680
OFF_BK, OFF_BV = 8192, 8320
IN_W = 8448

VMEM_LIMIT = 56 * 1024 * 1024


def _pair_major_heads(w, axis):
    shp = w.shape
    w = w.reshape(shp[:axis] + (KV_B, REP_B, DH) + shp[axis + 1:])
    return jnp.swapaxes(w, axis, axis + 1).reshape(shp)


def _permute_in_proj(w_in):
    sizes = dict(aq=512, ak=512, av=512, ag=512, bq=512, bk=128, bv=128, bg=512, cq=512, ck=512, cv=512,
                 cg=512, mg=3072)
    seg, off = {}, 0
    for name, size in sizes.items():
        seg[name] = w_in[:, :, off:off + size]
        off += size
    seg["bq"] = _pair_major_heads(seg["bq"], 2)
    seg["bg"] = _pair_major_heads(seg["bg"], 2)
    order = ("mg", "aq", "ak", "av", "ag", "bq", "bg", "cq", "ck", "cv", "cg", "bk", "bv")
    return jnp.concatenate([seg[n] for n in order], axis=2)


def _silu(x):
    return x * jax.nn.sigmoid(x)


def _dot_nt(a, b):
    return lax.dot_general(a, b, (((1,), (1,)), ((), ())), preferred_element_type=F32)


def _dot(a, b):
    return jnp.dot(a, b, preferred_element_type=F32)


def _pair_norm_rope(a, gain, cos, sin_lo, sin_hi, left):
    sq = a * a
    s_l = jnp.sum(jnp.where(left, sq, 0.0), axis=-1, keepdims=True)
    s_r = jnp.sum(jnp.where(left, 0.0, sq), axis=-1, keepdims=True)
    ms = jnp.where(left, s_l, s_r) * (1.0 / DH)
    y = a * lax.rsqrt(ms + EPS) * gain
    return y * cos + pltpu.roll(y, 112, 1) * sin_lo + pltpu.roll(y, 16, 1) * sin_hi


def _in_proj_kernel(x_ref, g_ref, w_ref, qn_ref, kn_ref, cos_ref, slo_ref, shi_ref, u_ref):
    x = x_ref[...]
    ms = jnp.mean(x * x, axis=-1, keepdims=True)
    h = (x * lax.rsqrt(ms + EPS) * g_ref[...]).astype(BF16)
    tm = x.shape[0]
    left = lax.broadcasted_iota(jnp.int32, (tm, LANES), 1) < DH
    cos, slo, shi = cos_ref[...], slo_ref[...], shi_ref[...]
    for c0 in range(0, OFF_BK, 512):
        acc = _dot(h, w_ref[:, c0:c0 + 512])
        if c0 == OFF_BQ:
            for b in range(4):
                blk = _pair_norm_rope(acc[:, b * LANES:(b + 1) * LANES], qn_ref[...], cos, slo, shi, left)
                u_ref[:, c0 + b * LANES:c0 + (b + 1) * LANES] = (blk * QK_SCALE).astype(BF16)
        else:
            u_ref[:, c0:c0 + 512] = acc.astype(BF16)
    acc = _dot(h, w_ref[:, OFF_BK:IN_W])
    bk = _pair_norm_rope(acc[:, :LANES], kn_ref[...], cos, slo, shi, left)
    u_ref[:, OFF_BK:OFF_BV] = bk.astype(BF16)
    u_ref[:, OFF_BV:IN_W] = acc[:, LANES:].astype(BF16)


def _in_proj(x, pre_g, w_in, qn, kn, rope, layer, seq, tm):
    n = x.shape[0]
    nseq = seq // tm
    cos, slo, shi = rope
    vec = lambda width: pl.BlockSpec((None, 1, width), lambda t: (layer, 0, 0))
    tab = pl.BlockSpec((tm, LANES), lambda t: (t % nseq, 0))
    return pl.pallas_call(
        _in_proj_kernel,
        grid=(n // tm,),
        in_specs=[
            pl.BlockSpec((tm, D_MODEL), lambda t: (t, 0)),
            vec(D_MODEL),
            pl.BlockSpec((None, D_MODEL, IN_W), lambda t: (layer, 0, 0), pipeline_mode=pl.Buffered(1)),
            vec(LANES), vec(LANES), tab, tab, tab,
        ],
        out_specs=pl.BlockSpec((tm, IN_W), lambda t: (t, 0)),
        out_shape=jax.ShapeDtypeStruct((n, IN_W), BF16),
        compiler_params=pltpu.CompilerParams(dimension_semantics=("arbitrary",),
                                             vmem_limit_bytes=VMEM_LIMIT),
        name="in_proj",
    )(x, pre_g, w_in, qn, kn, cos, slo, shi)


def _two_stage_scratch(tq, seq, tk):
    return [pltpu.VMEM((2, seq // tk, tk, tq), F32), pltpu.VMEM((2, SUBLANES, tq), F32)] * 2


def _two_stage_step(qs, k_ref, vt_ref, bias, s_w, m_w, s_r, m_r):
    _, nchunk, tk, tq = s_w.shape
    slabs = tk // SUBLANES
    m_run, acc = [None] * 2, [None] * 2
    for c in range(nchunk):
        kc = k_ref[c * tk:(c + 1) * tk, :]
        vtc = vt_ref[:, c * tk:(c + 1) * tk]
        for e in range(2):
            s = _dot_nt(kc, qs[e])
            if bias is not None:
                s = s + bias(c, e)
            s_w[e, c] = s
            m_c = jnp.max(s.reshape(slabs, SUBLANES, tq), axis=0)
            m_run[e] = m_c if c == 0 else jnp.maximum(m_run[e], m_c)
            p = jnp.exp(s_r[e, c].reshape(slabs, SUBLANES, tq) - m_r[e][None])
            pv = _dot(vtc, p.reshape(tk, tq).astype(BF16))
            acc[e] = pv if c == 0 else acc[e] + pv
    outs = []
    for e in range(2):
        m_w[e] = jnp.broadcast_to(jnp.max(m_run[e], axis=0, keepdims=True), (SUBLANES, tq))
        outs.append((acc[e][:LANES] / acc[e][LANES:LANES + 1]).T)
    return outs


def _two_stage(step, scratch):
    s_a, m_a, s_b, m_b = scratch
    t = pl.program_id(0)

    @pl.when(t == 0)
    def _():
        s_b[...] = jnp.zeros_like(s_b)
        m_b[...] = jnp.zeros_like(m_b)

    @pl.when(t % 2 == 0)
    def _():
        step(s_a, m_a, s_b, m_b)

    @pl.when(t % 2 == 1)
    def _():
        step(s_b, m_b, s_a, m_a)


def _lane_halves(q):
    left = lax.broadcasted_iota(jnp.int32, q.shape, 1) < DH
    zero = jnp.zeros_like(q)
    return jnp.where(left, q, zero), jnp.where(left, zero, q)


def _unit_maps(b, heads, n_i):
    units = b * heads * n_i

    def unit(t, lag):
        un = jnp.clip(t - lag, 0, units - 1)
        return un // (heads * n_i), (un // n_i) % heads, un % n_i

    return units, unit


def _diff_attn_kernel(lam_ref, sg_ref, q_ref, k_ref, vt_ref, g_ref, t_ref, o_ref, *scratch,
                      lambda_init, n_i, units):
    i = jnp.minimum(pl.program_id(0), units - 1) % n_i
    per_q = q_ref.shape[0] // t_ref.shape[2]

    def bias(c, mp):
        return t_ref[jnp.clip(c - i * per_q, -2, per_q + 1) + 2, mp]

    def step(*bufs):
        lp = lam_ref[...]
        lam = (jnp.exp(jnp.sum(lp[0:1] * lp[1:2], keepdims=True))
               - jnp.exp(jnp.sum(lp[2:3] * lp[3:4], keepdims=True)) + lambda_init)
        o0, o1 = _two_stage_step(_lane_halves(q_ref[...] * QK_SCALE), k_ref, vt_ref, bias, *bufs)
        o = o0 - lam * o1
        o = o * lax.rsqrt(jnp.mean(o * o, axis=-1, keepdims=True) + EPS) * sg_ref[...]
        o = o * (1.0 - lambda_init)
        o_ref[...] = (o * _silu(g_ref[...].astype(F32))).astype(BF16)

    _two_stage(step, scratch)


def _value_rows(v):
    b, seq, width = v.shape
    vt = jnp.transpose(v.reshape(b, seq, width // LANES, LANES), (0, 2, 3, 1))
    return jnp.concatenate([vt, jnp.ones((b, width // LANES, ONES_ROWS, seq), v.dtype)], axis=2)


def _diff_attn(u, lam_qk, subln_g, t5_tiles, layer, tq):
    b, seq, _ = u.shape
    ntile, _, tk, _ = t5_tiles.shape
    vt = _value_rows(u[:, :, OFF_AV:OFF_AV + W_BR])
    n_i = seq // tq
    units, unit = _unit_maps(b, H_A, n_i)
    lambda_init = 0.8 - 0.6 * math.exp(-0.3 * layer)

    def row_blk(off, lag):
        def index(t):
            bi, h, i = unit(t, lag)
            return bi, i, off // LANES + h
        return pl.BlockSpec((None, tq, LANES), index)

    def seq_blk(off, lag):
        def index(t):
            bi, h, _ = unit(t, lag)
            return bi, 0, off // LANES + h
        return pl.BlockSpec((None, seq, LANES), index)

    return pl.pallas_call(
        functools.partial(_diff_attn_kernel, lambda_init=lambda_init, n_i=n_i, units=units),
        grid=(units + 1,),
        in_specs=[
            pl.BlockSpec((None, 4, DH), lambda t: (layer, 0, 0)),
            pl.BlockSpec((None, 1, LANES), lambda t: (layer, 0, 0)),
            row_blk(OFF_AQ, 0), seq_blk(OFF_AK, 0),
            pl.BlockSpec((None, None, LANES + ONES_ROWS, seq), lambda t: unit(t, 1)[:2] + (0, 0)),
            row_blk(OFF_AG, 1),
            pl.BlockSpec((ntile, 2, tk, tq), lambda t: (0, unit(t, 0)[1], 0, 0)),
        ],
        out_specs=row_blk(0, 1),
        out_shape=jax.ShapeDtypeStruct((b, seq, W_BR), BF16),
        scratch_shapes=_two_stage_scratch(tq, seq, tk),
        compiler_params=pltpu.CompilerParams(dimension_semantics=("arbitrary",),
                                             vmem_limit_bytes=VMEM_LIMIT),
        name="diff_attn",
    )(lam_qk, subln_g, u, u, vt, u, t5_tiles)


def _gqa_kernel(q_ref, k_ref, vt_ref, g_ref, o_ref, *scratch):
    def step(*bufs):
        o0, o1 = _two_stage_step(_lane_halves(q_ref[...]), k_ref, vt_ref, None, *bufs)
        left = lax.broadcasted_iota(jnp.int32, o0.shape, 1) < DH
        o = jnp.where(left, o0, o1)
        o_ref[...] = (o * _silu(g_ref[...].astype(F32))).astype(BF16)

    _two_stage(step, scratch)


def _gqa_attn(u, tq):
    b, seq, _ = u.shape
    units, unit = _unit_maps(b, REP_B, seq // tq)

    def row_blk(off, lag):
        def index(t):
            bi, pr, i = unit(t, lag)
            return bi, i, off // LANES + pr
        return pl.BlockSpec((None, tq, LANES), index)

    def seq_blk(off, lag):
        return pl.BlockSpec((None, seq, LANES), lambda t: (unit(t, lag)[0], 0, off // LANES))

    vt = _value_rows(u[:, :, OFF_BV:OFF_BV + LANES])
    return pl.pallas_call(
        _gqa_kernel,
        grid=(units + 1,),
        in_specs=[
            row_blk(OFF_BQ, 0), seq_blk(OFF_BK, 0),
            pl.BlockSpec((None, None, LANES + ONES_ROWS, seq), lambda t: (unit(t, 1)[0], 0, 0, 0)),
            row_blk(OFF_BG, 1),
        ],
        out_specs=row_blk(0, 1),
        out_shape=jax.ShapeDtypeStruct((b, seq, W_BR), BF16),
        scratch_shapes=_two_stage_scratch(tq, seq, ATTN_TK),
        compiler_params=pltpu.CompilerParams(dimension_semantics=("arbitrary",),
                                             vmem_limit_bytes=VMEM_LIMIT),
        name="gqa_attn",
    )(u, u, vt, u)


def _na_block_type(bi, nblk):
    return jnp.where(bi < 2, bi, jnp.where(bi <= nblk - 3, 2, bi - (nblk - 5)))


def _na_kernel(q_ref, k_ref, v_ref, g_ref, tab_ref, o_ref, *, rows):
    big = pl.program_id(1)
    nsub = q_ref.shape[0] // NA_QBLK
    nblk = rows // 2
    nkeys = NA_BAND * GRID_W
    left = lax.broadcasted_iota(jnp.int32, (NA_QBLK, LANES), 1) < DH

    def slices(sub, pair):
        bi = big * nsub + sub
        b0 = jnp.clip(2 * bi - NA_ROWS // 2, 0, rows - NA_BAND)
        ks = pl.ds(pl.multiple_of(b0 * GRID_W, GRID_W), nkeys)
        return (_na_block_type(bi, nblk), ks, slice(sub * NA_QBLK, (sub + 1) * NA_QBLK),
                slice(pair * LANES, (pair + 1) * LANES))

    def scores(sub, pair):
        typ, ks, rs, cs = slices(sub, pair)
        qq = jnp.concatenate(_lane_halves(q_ref[rs, cs] * QK_SCALE), axis=0)
        bias = tab_ref[typ, 2 * pair:2 * pair + 2].reshape(2 * NA_QBLK, nkeys)
        return _dot_nt(qq, k_ref[ks, cs]) + bias

    def finish(sub, pair, s):
        _, ks, rs, cs = slices(sub, pair)
        m = jnp.max(s, axis=-1, keepdims=True)
        p = jnp.exp(s - m)
        l = jnp.sum(p, axis=-1, keepdims=True)
        o = _dot(p.astype(BF16), v_ref[ks, cs]) / l
        o = jnp.where(left, o[:NA_QBLK], o[NA_QBLK:])
        o_ref[rs, cs] = (o * _silu(g_ref[rs, cs].astype(F32))).astype(BF16)

    pending = None
    for sub in range(nsub):
        for pair in range(H_C // 2):
            s = scores(sub, pair)
            if pending is not None:
                finish(*pending)
            pending = (sub, pair, s)
    finish(*pending)


def _na_attn(u, na_tab, layer, tq):
    b, seq, _ = u.shape
    rows = seq // GRID_W
    row_blk = lambda off: pl.BlockSpec((None, tq, W_BR), lambda bi, i: (bi, i, off // W_BR))
    seq_blk = lambda off: pl.BlockSpec((None, seq, W_BR), lambda bi, i: (bi, 0, off // W_BR))
    return pl.pallas_call(
        functools.partial(_na_kernel, rows=rows),
        grid=(b, seq // tq),
        in_specs=[
            row_blk(OFF_CQ), seq_blk(OFF_CK), seq_blk(OFF_CV), row_blk(OFF_CG),
            pl.BlockSpec((None,) + na_tab.shape[1:], lambda bi, i: (layer, 0, 0, 0, 0),
                         pipeline_mode=pl.Buffered(1)),
        ],
        out_specs=pl.BlockSpec((None, tq, W_BR), lambda bi, i: (bi, i, 0)),
        out_shape=jax.ShapeDtypeStruct((b, seq, W_BR), BF16),
        compiler_params=pltpu.CompilerParams(dimension_semantics=("arbitrary", "arbitrary"),
                                             vmem_limit_bytes=VMEM_LIMIT),
        name="nbr_attn",
    )(u, u, u, u, na_tab)


def _merge_kernel(x_ref, ya_ref, yb_ref, yc_ref, mg_ref, p_ref, wa_ref, wb_ref, wc_ref, wo_ref,
                  wpg_ref, wpe_ref, pg_ref, o_ref):
    def gate(k):
        return jax.nn.sigmoid(mg_ref[:, k * D_MODEL:(k + 1) * D_MODEL].astype(F32))

    m = gate(0) * _dot(ya_ref[...], wa_ref[...])
    m = m + gate(1) * _dot(yb_ref[...], wb_ref[...])
    m = m + gate(2) * _dot(yc_ref[...], wc_ref[...])
    r = _dot(m.astype(BF16), wo_ref[...])
    r = r * lax.rsqrt(jnp.mean(r * r, axis=-1, keepdims=True) + EPS) * pg_ref[...]
    x = x_ref[...] + r
    emb = _dot(p_ref[...].astype(BF16), wpe_ref[...])
    o_ref[...] = x + jax.nn.sigmoid(_dot(x.astype(BF16), wpg_ref[...])) * emb


def _merge(x, ya, yb, yc, u, p, wa, wb, wc, wo, wpg, wpe, post_g, layer, tm):
    n = x.shape[0]
    tok = lambda width: pl.BlockSpec((tm, width), lambda t: (t, 0))
    wgt = lambda r, c: pl.BlockSpec((None, r, c), lambda t: (layer, 0, 0))
    return pl.pallas_call(
        _merge_kernel,
        grid=(n // tm,),
        in_specs=[
            tok(D_MODEL), tok(W_BR), tok(W_BR), tok(W_BR),
            pl.BlockSpec((tm, 3 * D_MODEL), lambda t: (t, OFF_MG // (3 * D_MODEL))),
            pl.BlockSpec((None, tm, D_PLE), lambda t: (layer, t, 0)),
            wgt(W_BR, D_MODEL), wgt(W_BR, D_MODEL), wgt(W_BR, D_MODEL),
            wgt(D_MODEL, D_MODEL), wgt(D_MODEL, D_MODEL), wgt(D_PLE, D_MODEL),
            wgt(1, D_MODEL),
        ],
        out_specs=tok(D_MODEL),
        out_shape=jax.ShapeDtypeStruct((n, D_MODEL), F32),
        compiler_params=pltpu.CompilerParams(dimension_semantics=("arbitrary",),
                                             vmem_limit_bytes=VMEM_LIMIT),
        name="merge",
    )(x, ya, yb, yc, u, p, wa, wb, wc, wo, wpg, wpe, post_g)


def _t5_bucket(rel):
    nb = T5_BUCKETS // 2
    ret = jnp.where(rel > 0, nb, 0)
    n = jnp.abs(rel)
    max_exact = nb // 2
    nf = jnp.maximum(n, 1).astype(F32)
    large = max_exact + (jnp.log(nf / max_exact) / math.log(T5_MAX_DIST / max_exact)
                         * (nb - max_exact)).astype(jnp.int32)
    large = jnp.minimum(large, nb - 1)
    return ret + jnp.where(n < max_exact, n, large)


def _t5_tiles(t5_table, seq, tq, tk):
    assert tk >= T5_MAX_DIST and tq % tk == 0
    rel = jnp.arange(-(seq - 1), seq)
    rel_bias = t5_table[_t5_bucket(rel)].astype(F32).T
    big = tq + (tq // tk + 2) * tk
    padded = jnp.pad(rel_bias, ((0, 0), (big, big)), mode="edge")
    period = tq + tk
    tiles = []
    for e in range(-2, tq // tk + 2):
        base = seq - 1 + big + e * tk
        seg = padded[:, base - tq:base + tk]
        w = jnp.concatenate([seg[:, tq:], seg[:, :tq]], axis=1)
        toe = jnp.tile(w, (1, tq))[:, :tq * (period - 1)].reshape(-1, tq, period - 1)
        tiles.append(toe[:, :, :tk])
    return jnp.swapaxes(jnp.stack(tiles), 2, 3)


def _rope_tables(seq):
    t = jnp.arange(seq)
    row = (t // GRID_W).astype(F32)
    col = (t % GRID_W).astype(F32)
    n_freq = DH // 4
    inv = ROPE_THETA ** (-jnp.arange(n_freq, dtype=F32) / n_freq)
    ang = jnp.concatenate([row[:, None] * inv] * 2 + [col[:, None] * inv] * 2, axis=1)
    cos, sin = jnp.cos(ang), jnp.sin(ang)
    first_half = (jnp.arange(DH) % (2 * n_freq)) < n_freq
    sin_lo = jnp.where(first_half, -sin, 0.0)
    sin_hi = jnp.where(first_half, 0.0, sin)
    two = lambda a: jnp.concatenate([a, a], axis=1)
    return two(cos), two(sin_lo), two(sin_hi)


def _na_tables(rpb, seq):
    rows = seq // GRID_W
    nblk = rows // 2
    assert rows >= NA_BAND and nblk >= 5
    wr = min(NA_ROWS, rows)
    depth = rpb.shape[0]
    ncol = 2 * NA_COLS - 1
    bis = jnp.array([0, 1, 2, nblk - 2, nblk - 1])[:, None, None, None, None]
    qr = 2 * bis + jnp.arange(2)[None, :, None, None, None]
    qc = jnp.arange(GRID_W)[None, None, :, None, None]
    kr = (jnp.clip(2 * bis - NA_ROWS // 2, 0, rows - NA_BAND)
          + jnp.arange(NA_BAND)[None, None, None, :, None])
    kc = jnp.arange(GRID_W)[None, None, None, None, :]
    sr = jnp.clip(qr - wr // 2, 0, rows - wr)
    sc = jnp.clip(qc - NA_COLS // 2, 0, GRID_W - NA_COLS)
    mask = (kr >= sr) & (kr < sr + wr) & (kc >= sc) & (kc < sc + NA_COLS)
    ri = jnp.clip(kr - qr + NA_ROWS - 1, 0, 2 * NA_ROWS - 2)[:, :, 0, :, 0]
    ci = jnp.clip(kc - qc + NA_COLS - 1, 0, ncol - 1)[0, 0, :, 0, :]
    by_row = rpb.astype(F32)[:, :, ri, :]
    onehot = (jnp.arange(ncol)[:, None] == ci.reshape(1, -1)).astype(F32)
    bias = jnp.dot(by_row.reshape(-1, ncol), onehot, precision=lax.Precision.HIGHEST)
    bias = bias.reshape(depth, H_C, 5, 2, NA_BAND, GRID_W, GRID_W)
    bias = jnp.transpose(bias, (0, 2, 1, 3, 5, 4, 6))
    bias = jnp.where(mask[None, :, None], bias, -jnp.inf)
    return bias.reshape(depth, 5, H_C, NA_QBLK, NA_BAND * GRID_W)


def _prepare(t5_table, pre_norm_g, w_in, subln_g, q_norm_g, k_norm_g, na_rpb, w_branch_a, w_branch_b,
             w_branch_c, w_out, post_norm_g, w_ple_proj, w_ple_gate, seq, tq):
    depth = w_in.shape[0]
    two = lambda a: jnp.concatenate([a, a], axis=-1).reshape(depth, 1, LANES)
    return dict(
        w_in=_permute_in_proj(w_in).astype(BF16),
        pre_g=pre_norm_g.reshape(depth, 1, D_MODEL),
        qn=two(q_norm_g), kn=two(k_norm_g),
        subln=subln_g.reshape(depth, 1, LANES),
        wa=w_branch_a.astype(BF16), wb=_pair_major_heads(w_branch_b, 1).astype(BF16), wc=w_branch_c.astype(BF16),
        wo=w_out.astype(BF16), wpg=w_ple_gate.astype(BF16), wpe=w_ple_proj.astype(BF16),
        post_g=post_norm_g.reshape(depth, 1, D_MODEL),
        t5=_t5_tiles(t5_table, seq, tq, T5_TK), rope=_rope_tables(seq), na=_na_tables(na_rpb, seq),
    )


def _trunk(x, p, lambda_qk, w, *, tm, tq, tq_c):
    b, seq, _ = x.shape
    depth = p.shape[0]
    n = b * seq
    xf = x.reshape(n, D_MODEL)
    pf = p.reshape(depth, n, D_PLE)
    for layer in range(depth):
        u = _in_proj(xf, w["pre_g"], w["w_in"], w["qn"], w["kn"], w["rope"], layer, seq, tm)
        u3 = u.reshape(b, seq, IN_W)
        ya = _diff_attn(u3, lambda_qk, w["subln"], w["t5"], layer, tq)
        yb = _gqa_attn(u3, tq)
        yc = _na_attn(u3, w["na"], layer, tq_c)
        xf = _merge(xf, ya.reshape(n, W_BR), yb.reshape(n, W_BR), yc.reshape(n, W_BR), u, pf,
                    w["wa"], w["wb"], w["wc"], w["wo"], w["wpg"], w["wpe"], w["post_g"], layer, tm)
    return xf.reshape(b, seq, D_MODEL)


TILES = dict(tm=512, tq=256, tq_c=512)


def kernel(x_prompt, x_sample, p_prompt, p_sample, t5_table, pre_norm_g, w_in, lambda_qk, subln_g, q_norm_g,
           k_norm_g, na_rpb, w_branch_a, w_branch_b, w_branch_c, w_out, post_norm_g, w_ple_proj, w_ple_gate):
    assert x_prompt.shape[1] == x_sample.shape[1]
    w = _prepare(t5_table, pre_norm_g, w_in, subln_g, q_norm_g, k_norm_g, na_rpb, w_branch_a, w_branch_b,
                 w_branch_c, w_out, post_norm_g, w_ple_proj, w_ple_gate, x_prompt.shape[1], TILES["tq"])
    y_prompt = _trunk(x_prompt, p_prompt, lambda_qk, w, **TILES)
    y_sample = _trunk(x_sample, p_sample, lambda_qk, w, **TILES)
    return (y_prompt, y_sample)
```

```python
import functools
import math

import jax
import jax.numpy as jnp
from jax import lax
from jax.experimental import pallas as pl
from jax.experimental.pallas import tpu as pltpu

F32 = jnp.float32
BF16 = jnp.bfloat16

D_MODEL = 1024
D_PLE = 256
EPS = 1e-6
GRID_W = 64
LANES = 128
SUBLANES = 8
ONES_ROWS = 16
DH = 64
H_A = 4
H_B = 8
KV_B = 2
REP_B = H_B // KV_B
H_C = 8
W_BR = 512
NA_ROWS = 8
NA_COLS = 16
NA_BAND = 10
NA_QBLK = 2 * GRID_W
T5_BUCKETS = 32
T5_MAX_DIST = 128
ATTN_TK = 256
T5_TK = ATTN_TK
ROPE_THETA = 10000.0
LOG2E = math.log2(math.e)
QK_SCALE = DH ** -0.5 * LOG2E

OFF_MG = 0
OFF_AQ, OFF_AK, OFF_AV, OFF_AG = 3072, 3584, 4096, 4608
OFF_BQ, OFF_BG = 5120, 5632
OFF_CQ, OFF_CK, OFF_CV, OFF_CG = 6144, 6656, 7168, 7680
OFF_BK, OFF_BV = 8192, 8320
IN_W = 8448

VMEM_LIMIT = 56 * 1024 * 1024


def _pair_major_heads(w, axis):
    shp = w.shape
    w = w.reshape(shp[:axis] + (KV_B, REP_B, DH) + shp[axis + 1:])
    return jnp.swapaxes(w, axis, axis + 1).reshape(shp)


def _permute_in_proj(w_in):
    sizes = dict(aq=512, ak=512, av=512, ag=512, bq=512, bk=128, bv=128, bg=512, cq=512, ck=512, cv=512,
                 cg=512, mg=3072)
    seg, off = {}, 0
    for name, size in sizes.items():
        seg[name] = w_in[:, :, off:off + size]
        off += size
    seg["aq"] = seg["aq"] * QK_SCALE
    seg["cq"] = seg["cq"] * QK_SCALE
    seg["bq"] = _pair_major_heads(seg["bq"], 2)
    seg["bg"] = _pair_major_heads(seg["bg"], 2)
    order = ("mg", "aq", "ak", "av", "ag", "bq", "bg", "cq", "ck", "cv", "cg", "bk", "bv")
    return jnp.concatenate([seg[n] for n in order], axis=2)


def _silu(x):
    return x * jax.nn.sigmoid(x)


def _dot_nt(a, b):
    return lax.dot_general(a, b, (((1,), (1,)), ((), ())), preferred_element_type=F32)


def _dot(a, b):
    return jnp.dot(a, b, preferred_element_type=F32)


def _pair_norm_rope(a, gain, cos, sin_lo, sin_hi, left):
    sq = a * a
    s_l = jnp.sum(jnp.where(left, sq, 0.0), axis=-1, keepdims=True)
    s_r = jnp.sum(jnp.where(left, 0.0, sq), axis=-1, keepdims=True)
    ms = jnp.where(left, s_l, s_r) * (1.0 / DH)
    y = a * lax.rsqrt(ms + EPS) * gain
    return y * cos + pltpu.roll(y, 112, 1) * sin_lo + pltpu.roll(y, 16, 1) * sin_hi


def _in_proj_kernel(x_ref, g_ref, w_ref, qn_ref, kn_ref, cos_ref, slo_ref, shi_ref, u_ref):
    x = x_ref[...]
    ms = jnp.mean(x * x, axis=-1, keepdims=True)
    h = (x * lax.rsqrt(ms + EPS) * g_ref[...]).astype(BF16)
    tm = x.shape[0]
    left = lax.broadcasted_iota(jnp.int32, (tm, LANES), 1) < DH
    cos, slo, shi = cos_ref[...], slo_ref[...], shi_ref[...]
    for c0 in range(0, OFF_BK, 512):
        acc = _dot(h, w_ref[:, c0:c0 + 512])
        if c0 == OFF_BQ:
            for b in range(4):
                blk = _pair_norm_rope(acc[:, b * LANES:(b + 1) * LANES], qn_ref[...], cos, slo, shi, left)
                u_ref[:, c0 + b * LANES:c0 + (b + 1) * LANES] = (blk * QK_SCALE).astype(BF16)
        else:
            u_ref[:, c0:c0 + 512] = acc.astype(BF16)
    acc = _dot(h, w_ref[:, OFF_BK:IN_W])
    bk = _pair_norm_rope(acc[:, :LANES], kn_ref[...], cos, slo, shi, left)
    u_ref[:, OFF_BK:OFF_BV] = bk.astype(BF16)
    u_ref[:, OFF_BV:IN_W] = acc[:, LANES:].astype(BF16)


def _in_proj(x, pre_g, w_in, qn, kn, rope, layer, seq, tm):
    n = x.shape[0]
    nseq = seq // tm
    cos, slo, shi = rope
    vec = lambda width: pl.BlockSpec((None, 1, width), lambda t: (layer, 0, 0))
    tab = pl.BlockSpec((tm, LANES), lambda t: (t % nseq, 0))
    return pl.pallas_call(
        _in_proj_kernel,
        grid=(n // tm,),
        in_specs=[
            pl.BlockSpec((tm, D_MODEL), lambda t: (t, 0)),
            vec(D_MODEL),
            pl.BlockSpec((None, D_MODEL, IN_W), lambda t: (layer, 0, 0), pipeline_mode=pl.Buffered(1)),
            vec(LANES), vec(LANES), tab, tab, tab,
        ],
        out_specs=pl.BlockSpec((tm, IN_W), lambda t: (t, 0)),
        out_shape=jax.ShapeDtypeStruct((n, IN_W), BF16),
        compiler_params=pltpu.CompilerParams(dimension_semantics=("arbitrary",),
                                             vmem_limit_bytes=VMEM_LIMIT),
        name="in_proj",
    )(x, pre_g, w_in, qn, kn, cos, slo, shi)


def _two_stage_scratch(tq, seq, tk):
    return [pltpu.VMEM((2, seq // tk, tk, tq), F32), pltpu.VMEM((2, SUBLANES, tq), F32)] * 2


def _two_stage_step(qs, k_ref, vt_ref, bias, s_w, m_w, s_r, m_r):
    _, nchunk, tk, tq = s_w.shape
    slabs = tk // SUBLANES
    m_run, acc = [None] * 2, [None] * 2
    for c in range(nchunk):
        kc = k_ref[c * tk:(c + 1) * tk, :]
        vtc = vt_ref[:, c * tk:(c + 1) * tk]
        for e in range(2):
            s = _dot_nt(kc, qs[e])
            if bias is not None:
                s = s + bias(c, e)
            s_w[e, c] = s
            m_c = jnp.max(s.reshape(slabs, SUBLANES, tq), axis=0)
            m_run[e] = m_c if c == 0 else jnp.maximum(m_run[e], m_c)
            p = jnp.exp2(s_r[e, c].reshape(slabs, SUBLANES, tq) - m_r[e][None])
            pv = _dot(vtc, p.reshape(tk, tq).astype(BF16))
            acc[e] = pv if c == 0 else acc[e] + pv
    outs = []
    for e in range(2):
        m_w[e] = jnp.broadcast_to(jnp.max(m_run[e], axis=0, keepdims=True), (SUBLANES, tq))
        outs.append((acc[e][:LANES] / acc[e][LANES:LANES + 1]).T)
    return outs


def _two_stage(step, scratch):
    s_a, m_a, s_b, m_b = scratch
    t = pl.program_id(0)

    @pl.when(t == 0)
    def _():
        s_b[...] = jnp.zeros_like(s_b)
        m_b[...] = jnp.zeros_like(m_b)

    @pl.when(t % 2 == 0)
    def _():
        step(s_a, m_a, s_b, m_b)

    @pl.when(t % 2 == 1)
    def _():
        step(s_b, m_b, s_a, m_a)


def _lane_halves(q):
    left = lax.broadcasted_iota(jnp.int32, q.shape, 1) < DH
    zero = jnp.zeros_like(q)
    return jnp.where(left, q, zero), jnp.where(left, zero, q)


def _unit_maps(b, heads, n_i):
    units = b * heads * n_i

    def unit(t, lag):
        un = jnp.clip(t - lag, 0, units - 1)
        return un // (heads * n_i), (un // n_i) % heads, un % n_i

    return units, unit


def _diff_attn_kernel(lam_ref, sg_ref, q_ref, k_ref, vt_ref, g_ref, t_ref, o_ref, *scratch,
                      lambda_init, n_i, units):
    i = jnp.minimum(pl.program_id(0), units - 1) % n_i
    per_q = q_ref.shape[0] // t_ref.shape[2]

    def bias(c, mp):
        return t_ref[jnp.clip(c - i * per_q, -2, per_q + 1) + 2, mp]

    def step(*bufs):
        lp = lam_ref[...]
        lam = (jnp.exp(jnp.sum(lp[0:1] * lp[1:2], keepdims=True))
               - jnp.exp(jnp.sum(lp[2:3] * lp[3:4], keepdims=True)) + lambda_init)
        o0, o1 = _two_stage_step(_lane_halves(q_ref[...]), k_ref, vt_ref, bias, *bufs)
        o = o0 - lam * o1
        o = o * lax.rsqrt(jnp.mean(o * o, axis=-1, keepdims=True) + EPS) * sg_ref[...]
        o = o * (1.0 - lambda_init)
        o_ref[...] = (o * _silu(g_ref[...].astype(F32))).astype(BF16)

    _two_stage(step, scratch)


def _value_rows(v):
    b, seq, width = v.shape
    vt = jnp.transpose(v.reshape(b, seq, width // LANES, LANES), (0, 2, 3, 1))
    return jnp.concatenate([vt, jnp.ones((b, width // LANES, ONES_ROWS, seq), v.dtype)], axis=2)


def _diff_attn(u, lam_qk, subln_g, t5_tiles, layer, tq):
    b, seq, _ = u.shape
    ntile, _, tk, _ = t5_tiles.shape
    vt = _value_rows(u[:, :, OFF_AV:OFF_AV + W_BR])
    n_i = seq // tq
    units, unit = _unit_maps(b, H_A, n_i)
    lambda_init = 0.8 - 0.6 * math.exp(-0.3 * layer)

    def row_blk(off, lag):
        def index(t):
            bi, h, i = unit(t, lag)
            return bi, i, off // LANES + h
        return pl.BlockSpec((None, tq, LANES), index)

    def seq_blk(off, lag):
        def index(t):
            bi, h, _ = unit(t, lag)
            return bi, 0, off // LANES + h
        return pl.BlockSpec((None, seq, LANES), index)

    return pl.pallas_call(
        functools.partial(_diff_attn_kernel, lambda_init=lambda_init, n_i=n_i, units=units),
        grid=(units + 1,),
        in_specs=[
            pl.BlockSpec((None, 4, DH), lambda t: (layer, 0, 0)),
            pl.BlockSpec((None, 1, LANES), lambda t: (layer, 0, 0)),
            row_blk(OFF_AQ, 0), seq_blk(OFF_AK, 0),
            pl.BlockSpec((None, None, LANES + ONES_ROWS, seq), lambda t: unit(t, 1)[:2] + (0, 0)),
            row_blk(OFF_AG, 1),
            pl.BlockSpec((ntile, 2, tk, tq), lambda t: (0, unit(t, 0)[1], 0, 0)),
        ],
        out_specs=row_blk(0, 1),
        out_shape=jax.ShapeDtypeStruct((b, seq, W_BR), BF16),
        scratch_shapes=_two_stage_scratch(tq, seq, tk),
        compiler_params=pltpu.CompilerParams(dimension_semantics=("arbitrary",),
                                             vmem_limit_bytes=VMEM_LIMIT),
        name="diff_attn",
    )(lam_qk, subln_g, u, u, vt, u, t5_tiles)


def _gqa_kernel(q_ref, k_ref, vt_ref, g_ref, o_ref, *scratch):
    def step(*bufs):
        o0, o1 = _two_stage_step(_lane_halves(q_ref[...]), k_ref, vt_ref, None, *bufs)
        left = lax.broadcasted_iota(jnp.int32, o0.shape, 1) < DH
        o = jnp.where(left, o0, o1)
        o_ref[...] = (o * _silu(g_ref[...].astype(F32))).astype(BF16)

    _two_stage(step, scratch)


def _gqa_attn(u, tq):
    b, seq, _ = u.shape
    units, unit = _unit_maps(b, REP_B, seq // tq)

    def row_blk(off, lag):
        def index(t):
            bi, pr, i = unit(t, lag)
            return bi, i, off // LANES + pr
        return pl.BlockSpec((None, tq, LANES), index)

    def seq_blk(off, lag):
        return pl.BlockSpec((None, seq, LANES), lambda t: (unit(t, lag)[0], 0, off // LANES))

    vt = _value_rows(u[:, :, OFF_BV:OFF_BV + LANES])
    return pl.pallas_call(
        _gqa_kernel,
        grid=(units + 1,),
        in_specs=[
            row_blk(OFF_BQ, 0), seq_blk(OFF_BK, 0),
            pl.BlockSpec((None, None, LANES + ONES_ROWS, seq), lambda t: (unit(t, 1)[0], 0, 0, 0)),
            row_blk(OFF_BG, 1),
        ],
        out_specs=row_blk(0, 1),
        out_shape=jax.ShapeDtypeStruct((b, seq, W_BR), BF16),
        scratch_shapes=_two_stage_scratch(tq, seq, ATTN_TK),
        compiler_params=pltpu.CompilerParams(dimension_semantics=("arbitrary",),
                                             vmem_limit_bytes=VMEM_LIMIT),
        name="gqa_attn",
    )(u, u, vt, u)


def _na_block_type(bi, nblk):
    return jnp.where(bi < 2, bi, jnp.where(bi <= nblk - 3, 2, bi - (nblk - 5)))


def _na_kernel(q_ref, k_ref, v_ref, g_ref, tab_ref, o_ref, *, rows):
    big = pl.program_id(1)
    nsub = q_ref.shape[0] // NA_QBLK
    nblk = rows // 2
    nkeys = NA_BAND * GRID_W
    left = lax.broadcasted_iota(jnp.int32, (NA_QBLK, LANES), 1) < DH

    def slices(sub, pair):
        bi = big * nsub + sub
        b0 = jnp.clip(2 * bi - NA_ROWS // 2, 0, rows - NA_BAND)
        ks = pl.ds(pl.multiple_of(b0 * GRID_W, GRID_W), nkeys)
        return (_na_block_type(bi, nblk), ks, slice(sub * NA_QBLK, (sub + 1) * NA_QBLK),
                slice(pair * LANES, (pair + 1) * LANES))

    def scores(sub, pair):
        typ, ks, rs, cs = slices(sub, pair)
        qq = jnp.concatenate(_lane_halves(q_ref[rs, cs]), axis=0)
        bias = tab_ref[typ, 2 * pair:2 * pair + 2].reshape(2 * NA_QBLK, nkeys)
        return _dot_nt(qq, k_ref[ks, cs]) + bias

    def finish(sub, pair, s):
        _, ks, rs, cs = slices(sub, pair)
        m = jnp.max(s, axis=-1, keepdims=True)
        p = jnp.exp2(s - m)
        l = jnp.sum(p, axis=-1, keepdims=True)
        o = _dot(p.astype(BF16), v_ref[ks, cs]) / l
        o = jnp.where(left, o[:NA_QBLK], o[NA_QBLK:])
        o_ref[rs, cs] = (o * _silu(g_ref[rs, cs].astype(F32))).astype(BF16)

    pending = None
    for sub in range(nsub):
        for pair in range(H_C // 2):
            s = scores(sub, pair)
            if pending is not None:
                finish(*pending)
            pending = (sub, pair, s)
    finish(*pending)


def _na_attn(u, na_tab, layer, tq):
    b, seq, _ = u.shape
    rows = seq // GRID_W
    row_blk = lambda off: pl.BlockSpec((None, tq, W_BR), lambda bi, i: (bi, i, off // W_BR))
    seq_blk = lambda off: pl.BlockSpec((None, seq, W_BR), lambda bi, i: (bi, 0, off // W_BR))
    return pl.pallas_call(
        functools.partial(_na_kernel, rows=rows),
        grid=(b, seq // tq),
        in_specs=[
            row_blk(OFF_CQ), seq_blk(OFF_CK), seq_blk(OFF_CV), row_blk(OFF_CG),
            pl.BlockSpec((None,) + na_tab.shape[1:], lambda bi, i: (layer, 0, 0, 0, 0),
                         pipeline_mode=pl.Buffered(1)),
        ],
        out_specs=pl.BlockSpec((None, tq, W_BR), lambda bi, i: (bi, i, 0)),
        out_shape=jax.ShapeDtypeStruct((b, seq, W_BR), BF16),
        compiler_params=pltpu.CompilerParams(dimension_semantics=("arbitrary", "arbitrary"),
                                             vmem_limit_bytes=VMEM_LIMIT),
        name="nbr_attn",
    )(u, u, u, u, na_tab)


def _merge_kernel(x_ref, ya_ref, yb_ref, yc_ref, mg_ref, p_ref, wa_ref, wb_ref, wc_ref, wo_ref,
                  wpg_ref, wpe_ref, pg_ref, o_ref):
    def gate(k):
        return jax.nn.sigmoid(mg_ref[:, k * D_MODEL:(k + 1) * D_MODEL].astype(F32))

    m = gate(0) * _dot(ya_ref[...], wa_ref[...])
    m = m + gate(1) * _dot(yb_ref[...], wb_ref[...])
    m = m + gate(2) * _dot(yc_ref[...], wc_ref[...])
    r = _dot(m.astype(BF16), wo_ref[...])
    r = r * lax.rsqrt(jnp.mean(r * r, axis=-1, keepdims=True) + EPS) * pg_ref[...]
    x = x_ref[...] + r
    emb = _dot(p_ref[...].astype(BF16), wpe_ref[...])
    o_ref[...] = x + jax.nn.sigmoid(_dot(x.astype(BF16), wpg_ref[...])) * emb


def _merge(x, ya, yb, yc, u, p, wa, wb, wc, wo, wpg, wpe, post_g, layer, tm):
    n = x.shape[0]
    tok = lambda width: pl.BlockSpec((tm, width), lambda t: (t, 0))
    wgt = lambda r, c: pl.BlockSpec((None, r, c), lambda t: (layer, 0, 0))
    return pl.pallas_call(
        _merge_kernel,
        grid=(n // tm,),
        in_specs=[
            tok(D_MODEL), tok(W_BR), tok(W_BR), tok(W_BR),
            pl.BlockSpec((tm, 3 * D_MODEL), lambda t: (t, OFF_MG // (3 * D_MODEL))),
            pl.BlockSpec((None, tm, D_PLE), lambda t: (layer, t, 0)),
            wgt(W_BR, D_MODEL), wgt(W_BR, D_MODEL), wgt(W_BR, D_MODEL),
            wgt(D_MODEL, D_MODEL), wgt(D_MODEL, D_MODEL), wgt(D_PLE, D_MODEL),
            wgt(1, D_MODEL),
        ],
        out_specs=tok(D_MODEL),
        out_shape=jax.ShapeDtypeStruct((n, D_MODEL), F32),
        compiler_params=pltpu.CompilerParams(dimension_semantics=("arbitrary",),
                                             vmem_limit_bytes=VMEM_LIMIT),
        name="merge",
    )(x, ya, yb, yc, u, p, wa, wb, wc, wo, wpg, wpe, post_g)


def _t5_bucket(rel):
    nb = T5_BUCKETS // 2
    ret = jnp.where(rel > 0, nb, 0)
    n = jnp.abs(rel)
    max_exact = nb // 2
    nf = jnp.maximum(n, 1).astype(F32)
    large = max_exact + (jnp.log(nf / max_exact) / math.log(T5_MAX_DIST / max_exact)
                         * (nb - max_exact)).astype(jnp.int32)
    large = jnp.minimum(large, nb - 1)
    return ret + jnp.where(n < max_exact, n, large)


def _t5_tiles(t5_table, seq, tq, tk):
    assert tk >= T5_MAX_DIST and tq % tk == 0
    rel = jnp.arange(-(seq - 1), seq)
    rel_bias = t5_table[_t5_bucket(rel)].astype(F32).T * LOG2E
    big = tq + (tq // tk + 2) * tk
    padded = jnp.pad(rel_bias, ((0, 0), (big, big)), mode="edge")
    period = tq + tk
    tiles = []
    for e in range(-2, tq // tk + 2):
        base = seq - 1 + big + e * tk
        seg = padded[:, base - tq:base + tk]
        w = jnp.concatenate([seg[:, tq:], seg[:, :tq]], axis=1)
        toe = jnp.tile(w, (1, tq))[:, :tq * (period - 1)].reshape(-1, tq, period - 1)
        tiles.append(toe[:, :, :tk])
    return jnp.swapaxes(jnp.stack(tiles), 2, 3)


def _rope_tables(seq):
    t = jnp.arange(seq)
    row = (t // GRID_W).astype(F32)
    col = (t % GRID_W).astype(F32)
    n_freq = DH // 4
    inv = ROPE_THETA ** (-jnp.arange(n_freq, dtype=F32) / n_freq)
    ang = jnp.concatenate([row[:, None] * inv] * 2 + [col[:, None] * inv] * 2, axis=1)
    cos, sin = jnp.cos(ang), jnp.sin(ang)
    first_half = (jnp.arange(DH) % (2 * n_freq)) < n_freq
    sin_lo = jnp.where(first_half, -sin, 0.0)
    sin_hi = jnp.where(first_half, 0.0, sin)
    two = lambda a: jnp.concatenate([a, a], axis=1)
    return two(cos), two(sin_lo), two(sin_hi)


def _na_tables(rpb, seq):
    rows = seq // GRID_W
    nblk = rows // 2
    assert rows >= NA_BAND and nblk >= 5
    wr = min(NA_ROWS, rows)
    depth = rpb.shape[0]
    ncol = 2 * NA_COLS - 1
    bis = jnp.array([0, 1, 2, nblk - 2, nblk - 1])[:, None, None, None, None]
    qr = 2 * bis + jnp.arange(2)[None, :, None, None, None]
    qc = jnp.arange(GRID_W)[None, None, :, None, None]
    kr = (jnp.clip(2 * bis - NA_ROWS // 2, 0, rows - NA_BAND)
          + jnp.arange(NA_BAND)[None, None, None, :, None])
    kc = jnp.arange(GRID_W)[None, None, None, None, :]
    sr = jnp.clip(qr - wr // 2, 0, rows - wr)
    sc = jnp.clip(qc - NA_COLS // 2, 0, GRID_W - NA_COLS)
    mask = (kr >= sr) & (kr < sr + wr) & (kc >= sc) & (kc < sc + NA_COLS)
    ri = jnp.clip(kr - qr + NA_ROWS - 1, 0, 2 * NA_ROWS - 2)[:, :, 0, :, 0]
    ci = jnp.clip(kc - qc + NA_COLS - 1, 0, ncol - 1)[0, 0, :, 0, :]
    by_row = rpb.astype(F32)[:, :, ri, :]
    onehot = (jnp.arange(ncol)[:, None] == ci.reshape(1, -1)).astype(F32)
    bias = jnp.dot(by_row.reshape(-1, ncol), onehot, precision=lax.Precision.HIGHEST)
    bias = bias.reshape(depth, H_C, 5, 2, NA_BAND, GRID_W, GRID_W)
    bias = jnp.transpose(bias, (0, 2, 1, 3, 5, 4, 6))
    bias = jnp.where(mask[None, :, None], bias * LOG2E, -jnp.inf)
    return bias.reshape(depth, 5, H_C, NA_QBLK, NA_BAND * GRID_W)


def _prepare(t5_table, pre_norm_g, w_in, subln_g, q_norm_g, k_norm_g, na_rpb, w_branch_a, w_branch_b,
             w_branch_c, w_out, post_norm_g, w_ple_proj, w_ple_gate, seq, tq):
    depth = w_in.shape[0]
    two = lambda a: jnp.concatenate([a, a], axis=-1).reshape(depth, 1, LANES)
    return dict(
        w_in=_permute_in_proj(w_in).astype(BF16),
        pre_g=pre_norm_g.reshape(depth, 1, D_MODEL),
        qn=two(q_norm_g), kn=two(k_norm_g),
        subln=subln_g.reshape(depth, 1, LANES),
        wa=w_branch_a.astype(BF16), wb=_pair_major_heads(w_branch_b, 1).astype(BF16), wc=w_branch_c.astype(BF16),
        wo=w_out.astype(BF16), wpg=w_ple_gate.astype(BF16), wpe=w_ple_proj.astype(BF16),
        post_g=post_norm_g.reshape(depth, 1, D_MODEL),
        t5=_t5_tiles(t5_table, seq, tq, T5_TK), rope=_rope_tables(seq), na=_na_tables(na_rpb, seq),
    )


def _trunk(x, p, lambda_qk, w, *, tm, tq_a, tq_b, tq_c):
    b, seq, _ = x.shape
    depth = p.shape[0]
    n = b * seq
    xf = x.reshape(n, D_MODEL)
    pf = p.reshape(depth, n, D_PLE)
    for layer in range(depth):
        u = _in_proj(xf, w["pre_g"], w["w_in"], w["qn"], w["kn"], w["rope"], layer, seq, tm)
        u3 = u.reshape(b, seq, IN_W)
        ya = _diff_attn(u3, lambda_qk, w["subln"], w["t5"], layer, tq_a)
        yb = _gqa_attn(u3, tq_b)
        yc = _na_attn(u3, w["na"], layer, tq_c)
        xf = _merge(xf, ya.reshape(n, W_BR), yb.reshape(n, W_BR), yc.reshape(n, W_BR), u, pf,
                    w["wa"], w["wb"], w["wc"], w["wo"], w["wpg"], w["wpe"], w["post_g"], layer, tm)
    return xf.reshape(b, seq, D_MODEL)


TILES = dict(tm=512, tq_a=512, tq_b=512, tq_c=512)


def kernel(x_prompt, x_sample, p_prompt, p_sample, t5_table, pre_norm_g, w_in, lambda_qk, subln_g, q_norm_g,
           k_norm_g, na_rpb, w_branch_a, w_branch_b, w_branch_c, w_out, post_norm_g, w_ple_proj, w_ple_gate):
    assert x_prompt.shape[1] == x_sample.shape[1]
    w = _prepare(t5_table, pre_norm_g, w_in, subln_g, q_norm_g, k_norm_g, na_rpb, w_branch_a, w_branch_b,
                 w_branch_c, w_out, post_norm_g, w_ple_proj, w_ple_gate, x_prompt.shape[1], TILES["tq_a"])
    y_prompt = _trunk(x_prompt, p_prompt, lambda_qk, w, **TILES)
    y_sample = _trunk(x_sample, p_sample, lambda_qk, w, **TILES)
    return (y_prompt, y_sample)
```

```python
import functools
import math

import jax
import jax.numpy as jnp
from jax import lax
from jax.experimental import pallas as pl
from jax.experimental.pallas import tpu as pltpu

F32 = jnp.float32
BF16 = jnp.bfloat16

D_MODEL = 1024
D_PLE = 256
EPS = 1e-6
GRID_W = 64
LANES = 128
SUBLANES = 8
ONES_ROWS = 16
DH = 64
H_A = 4
H_B = 8
KV_B = 2
REP_B = H_B // KV_B
H_C = 8
W_BR = 512
NA_ROWS = 8
NA_COLS = 16
NA_BAND = 10
NA_QBLK = 2 * GRID_W
T5_BUCKETS = 32
T5_MAX_DIST = 128
ATTN_TK = 256
T5_TK = ATTN_TK
ROPE_THETA = 10000.0
LOG2E = math.log2(math.e)
QK_SCALE = DH ** -0.5 * LOG2E

OFF_MG = 0
OFF_AQ, OFF_AK, OFF_AV, OFF_AG = 3072, 3584, 4096, 4608
OFF_BQ, OFF_BG = 5120, 5632
OFF_CQ, OFF_CK, OFF_CV, OFF_CG = 6144, 6656, 7168, 7680
OFF_BK, OFF_BV = 8192, 8320
IN_W = 8448

VMEM_LIMIT = 56 * 1024 * 1024


def _pair_major_heads(w, axis):
    shp = w.shape
    w = w.reshape(shp[:axis] + (KV_B, REP_B, DH) + shp[axis + 1:])
    return jnp.swapaxes(w, axis, axis + 1).reshape(shp)


def _permute_in_proj(w_in):
    sizes = dict(aq=512, ak=512, av=512, ag=512, bq=512, bk=128, bv=128, bg=512, cq=512, ck=512, cv=512,
                 cg=512, mg=3072)
    seg, off = {}, 0
    for name, size in sizes.items():
        seg[name] = w_in[:, :, off:off + size]
        off += size
    seg["aq"] = seg["aq"] * QK_SCALE
    seg["cq"] = seg["cq"] * QK_SCALE
    seg["bq"] = _pair_major_heads(seg["bq"], 2)
    seg["bg"] = _pair_major_heads(seg["bg"], 2)
    order = ("mg", "aq", "ak", "av", "ag", "bq", "bg", "cq", "ck", "cv", "cg", "bk", "bv")
    return jnp.concatenate([seg[n] for n in order], axis=2)


def _silu(x):
    return x * jax.nn.sigmoid(x)


def _dot_nt(a, b):
    return lax.dot_general(a, b, (((1,), (1,)), ((), ())), preferred_element_type=F32)


def _dot(a, b):
    return jnp.dot(a, b, preferred_element_type=F32)


def _pair_norm_rope(a, gain, cos, sin_lo, sin_hi, left):
    sq = a * a
    s_l = jnp.sum(jnp.where(left, sq, 0.0), axis=-1, keepdims=True)
    s_r = jnp.sum(jnp.where(left, 0.0, sq), axis=-1, keepdims=True)
    ms = jnp.where(left, s_l, s_r) * (1.0 / DH)
    y = a * lax.rsqrt(ms + EPS) * gain
    return y * cos + pltpu.roll(y, 112, 1) * sin_lo + pltpu.roll(y, 16, 1) * sin_hi


def _in_proj_kernel(x_ref, g_ref, w_ref, qn_ref, kn_ref, cos_ref, slo_ref, shi_ref, u_ref):
    x = x_ref[...]
    ms = jnp.mean(x * x, axis=-1, keepdims=True)
    h = (x * lax.rsqrt(ms + EPS) * g_ref[...]).astype(BF16)
    tm = x.shape[0]
    left = lax.broadcasted_iota(jnp.int32, (tm, LANES), 1) < DH
    cos, slo, shi = cos_ref[...], slo_ref[...], shi_ref[...]
    for c0 in range(0, OFF_BK, 512):
        acc = _dot(h, w_ref[:, c0:c0 + 512])
        if c0 == OFF_BQ:
            for b in range(4):
                blk = _pair_norm_rope(acc[:, b * LANES:(b + 1) * LANES], qn_ref[...], cos, slo, shi, left)
                u_ref[:, c0 + b * LANES:c0 + (b + 1) * LANES] = (blk * QK_SCALE).astype(BF16)
        else:
            u_ref[:, c0:c0 + 512] = acc.astype(BF16)
    acc = _dot(h, w_ref[:, OFF_BK:IN_W])
    bk = _pair_norm_rope(acc[:, :LANES], kn_ref[...], cos, slo, shi, left)
    u_ref[:, OFF_BK:OFF_BV] = bk.astype(BF16)
    u_ref[:, OFF_BV:IN_W] = acc[:, LANES:].astype(BF16)


def _in_proj(x, pre_g, w_in, qn, kn, rope, layer, seq, tm):
    n = x.shape[0]
    nseq = seq // tm
    cos, slo, shi = rope
    vec = lambda width: pl.BlockSpec((None, 1, width), lambda t: (layer, 0, 0))
    tab = pl.BlockSpec((tm, LANES), lambda t: (t % nseq, 0))
    return pl.pallas_call(
        _in_proj_kernel,
        grid=(n // tm,),
        in_specs=[
            pl.BlockSpec((tm, D_MODEL), lambda t: (t, 0)),
            vec(D_MODEL),
            pl.BlockSpec((None, D_MODEL, IN_W), lambda t: (layer, 0, 0), pipeline_mode=pl.Buffered(1)),
            vec(LANES), vec(LANES), tab, tab, tab,
        ],
        out_specs=pl.BlockSpec((tm, IN_W), lambda t: (t, 0)),
        out_shape=jax.ShapeDtypeStruct((n, IN_W), BF16),
        compiler_params=pltpu.CompilerParams(dimension_semantics=("arbitrary",),
                                             vmem_limit_bytes=VMEM_LIMIT),
        name="in_proj",
    )(x, pre_g, w_in, qn, kn, cos, slo, shi)


def _two_stage_scratch(tq, seq, tk):
    return [pltpu.VMEM((2, seq // tk, tk, tq), F32), pltpu.VMEM((2, SUBLANES, tq), F32)] * 2


def _two_stage_step(qs, k_ref, vt_ref, bias, s_w, m_w, s_r, m_r):
    _, nchunk, tk, tq = s_w.shape
    slabs = tk // SUBLANES
    m_run, acc = [None] * 2, [None] * 2
    for c in range(nchunk):
        kc = k_ref[c * tk:(c + 1) * tk, :]
        vtc = _with_ones(vt_ref[:, c * tk:(c + 1) * tk])
        for e in range(2):
            s = _dot_nt(kc, qs[e])
            if bias is not None:
                s = s + bias(c, e)
            s_w[e, c] = s
            m_c = jnp.max(s.reshape(slabs, SUBLANES, tq), axis=0)
            m_run[e] = m_c if c == 0 else jnp.maximum(m_run[e], m_c)
            p = jnp.exp2(s_r[e, c].reshape(slabs, SUBLANES, tq) - m_r[e][None])
            pv = _dot(vtc, p.reshape(tk, tq).astype(BF16))
            acc[e] = pv if c == 0 else acc[e] + pv
    outs = []
    for e in range(2):
        m_w[e] = jnp.broadcast_to(jnp.max(m_run[e], axis=0, keepdims=True), (SUBLANES, tq))
        outs.append((acc[e][:LANES] / acc[e][LANES:LANES + 1]).T)
    return outs


def _two_stage(step, scratch):
    s_a, m_a, s_b, m_b = scratch
    t = pl.program_id(0)

    @pl.when(t == 0)
    def _():
        s_b[...] = jnp.zeros_like(s_b)
        m_b[...] = jnp.zeros_like(m_b)

    @pl.when(t % 2 == 0)
    def _():
        step(s_a, m_a, s_b, m_b)

    @pl.when(t % 2 == 1)
    def _():
        step(s_b, m_b, s_a, m_a)


def _lane_halves(q):
    left = lax.broadcasted_iota(jnp.int32, q.shape, 1) < DH
    zero = jnp.zeros_like(q)
    return jnp.where(left, q, zero), jnp.where(left, zero, q)


def _unit_maps(b, heads, n_i):
    units = b * heads * n_i

    def unit(t, lag):
        un = jnp.clip(t - lag, 0, units - 1)
        return un // (heads * n_i), (un // n_i) % heads, un % n_i

    return units, unit


def _diff_attn_kernel(lam_ref, sg_ref, q_ref, k_ref, vt_ref, g_ref, t_ref, o_ref, *scratch,
                      lambda_init, n_i, units):
    i = jnp.minimum(pl.program_id(0), units - 1) % n_i
    per_q = q_ref.shape[0] // t_ref.shape[2]

    def bias(c, mp):
        return t_ref[jnp.clip(c - i * per_q, -2, per_q + 1) + 2, mp]

    def step(*bufs):
        lp = lam_ref[...]
        lam = (jnp.exp(jnp.sum(lp[0:1] * lp[1:2], keepdims=True))
               - jnp.exp(jnp.sum(lp[2:3] * lp[3:4], keepdims=True)) + lambda_init)
        o0, o1 = _two_stage_step(_lane_halves(q_ref[...]), k_ref, vt_ref, bias, *bufs)
        o = o0 - lam * o1
        o = o * lax.rsqrt(jnp.mean(o * o, axis=-1, keepdims=True) + EPS) * sg_ref[...]
        o = o * (1.0 - lambda_init)
        o_ref[...] = (o * _silu(g_ref[...].astype(F32))).astype(BF16)

    _two_stage(step, scratch)


def _value_rows(v):
    b, seq, width = v.shape
    return jnp.transpose(v.reshape(b, seq, width // LANES, LANES), (0, 2, 3, 1))


def _with_ones(vt):
    return jnp.concatenate([vt, jnp.ones((ONES_ROWS, vt.shape[1]), vt.dtype)], axis=0)


def _diff_attn(u, lam_qk, subln_g, t5_tiles, layer, tq):
    b, seq, _ = u.shape
    ntile, _, tk, _ = t5_tiles.shape
    vt = _value_rows(u[:, :, OFF_AV:OFF_AV + W_BR])
    n_i = seq // tq
    units, unit = _unit_maps(b, H_A, n_i)
    lambda_init = 0.8 - 0.6 * math.exp(-0.3 * layer)

    def row_blk(off, lag):
        def index(t):
            bi, h, i = unit(t, lag)
            return bi, i, off // LANES + h
        return pl.BlockSpec((None, tq, LANES), index)

    def seq_blk(off, lag):
        def index(t):
            bi, h, _ = unit(t, lag)
            return bi, 0, off // LANES + h
        return pl.BlockSpec((None, seq, LANES), index)

    return pl.pallas_call(
        functools.partial(_diff_attn_kernel, lambda_init=lambda_init, n_i=n_i, units=units),
        grid=(units + 1,),
        in_specs=[
            pl.BlockSpec((None, 4, DH), lambda t: (layer, 0, 0)),
            pl.BlockSpec((None, 1, LANES), lambda t: (layer, 0, 0)),
            row_blk(OFF_AQ, 0), seq_blk(OFF_AK, 0),
            pl.BlockSpec((None, None, LANES, seq), lambda t: unit(t, 1)[:2] + (0, 0)),
            row_blk(OFF_AG, 1),
            pl.BlockSpec((ntile, 2, tk, tq), lambda t: (0, unit(t, 0)[1], 0, 0)),
        ],
        out_specs=row_blk(0, 1),
        out_shape=jax.ShapeDtypeStruct((b, seq, W_BR), BF16),
        scratch_shapes=_two_stage_scratch(tq, seq, tk),
        compiler_params=pltpu.CompilerParams(dimension_semantics=("arbitrary",),
                                             vmem_limit_bytes=VMEM_LIMIT),
        name="diff_attn",
    )(lam_qk, subln_g, u, u, vt, u, t5_tiles)


def _gqa_kernel(q_ref, k_ref, vt_ref, g_ref, o_ref, *scratch):
    def step(*bufs):
        o0, o1 = _two_stage_step(_lane_halves(q_ref[...]), k_ref, vt_ref, None, *bufs)
        left = lax.broadcasted_iota(jnp.int32, o0.shape, 1) < DH
        o = jnp.where(left, o0, o1)
        o_ref[...] = (o * _silu(g_ref[...].astype(F32))).astype(BF16)

    _two_stage(step, scratch)


def _gqa_attn(u, tq):
    b, seq, _ = u.shape
    units, unit = _unit_maps(b, REP_B, seq // tq)

    def row_blk(off, lag):
        def index(t):
            bi, pr, i = unit(t, lag)
            return bi, i, off // LANES + pr
        return pl.BlockSpec((None, tq, LANES), index)

    def seq_blk(off, lag):
        return pl.BlockSpec((None, seq, LANES), lambda t: (unit(t, lag)[0], 0, off // LANES))

    vt = _value_rows(u[:, :, OFF_BV:OFF_BV + LANES])
    return pl.pallas_call(
        _gqa_kernel,
        grid=(units + 1,),
        in_specs=[
            row_blk(OFF_BQ, 0), seq_blk(OFF_BK, 0),
            pl.BlockSpec((None, None, LANES, seq), lambda t: (unit(t, 1)[0], 0, 0, 0)),
            row_blk(OFF_BG, 1),
        ],
        out_specs=row_blk(0, 1),
        out_shape=jax.ShapeDtypeStruct((b, seq, W_BR), BF16),
        scratch_shapes=_two_stage_scratch(tq, seq, ATTN_TK),
        compiler_params=pltpu.CompilerParams(dimension_semantics=("arbitrary",),
                                             vmem_limit_bytes=VMEM_LIMIT),
        name="gqa_attn",
    )(u, u, vt, u)


def _na_block_type(bi, nblk):
    return jnp.where(bi < 2, bi, jnp.where(bi <= nblk - 3, 2, bi - (nblk - 5)))


def _na_kernel(q_ref, k_ref, vt_ref, g_ref, tab_ref, o_ref, *, rows):
    big = pl.program_id(1)
    nsub = q_ref.shape[0] // NA_QBLK
    nblk = rows // 2
    nkeys = NA_BAND * GRID_W
    left = lax.broadcasted_iota(jnp.int32, (NA_QBLK, LANES), 1) < DH

    def slices(sub, pair):
        bi = big * nsub + sub
        b0 = jnp.clip(2 * bi - NA_ROWS // 2, 0, rows - NA_BAND)
        ks = pl.ds(pl.multiple_of(b0 * GRID_W, 2 * GRID_W), nkeys)
        return (_na_block_type(bi, nblk), ks, slice(sub * NA_QBLK, (sub + 1) * NA_QBLK),
                slice(pair * LANES, (pair + 1) * LANES))

    def scores(sub, pair):
        typ, ks, rs, cs = slices(sub, pair)
        qq = jnp.concatenate(_lane_halves(q_ref[rs, cs]), axis=0)
        return _dot_nt(k_ref[ks, cs], qq) + tab_ref[typ, pair]

    def weights(s):
        return jnp.exp2(s - jnp.max(s, axis=0, keepdims=True)).astype(BF16)

    def finish(sub, pair, p):
        _, ks, rs, cs = slices(sub, pair)
        ot = _dot(_with_ones(vt_ref[pair, :, ks]), p)
        o = (ot[:LANES] / ot[LANES:LANES + 1]).T
        o = jnp.where(left, o[:NA_QBLK], o[NA_QBLK:])
        o_ref[rs, cs] = (o * _silu(g_ref[rs, cs].astype(F32))).astype(BF16)

    units = [(sub, pair) for sub in range(nsub) for pair in range(H_C // 2)]
    s_prev, p_prev = None, None
    for n in range(len(units) + 2):
        s_new = scores(*units[n]) if n < len(units) else None
        p_new = weights(s_prev) if s_prev is not None else None
        if p_prev is not None:
            finish(*units[n - 2], p_prev)
        s_prev, p_prev = s_new, p_new


def _na_attn(u, na_tab, layer, tq):
    b, seq, _ = u.shape
    rows = seq // GRID_W
    assert (rows - NA_BAND) % 2 == 0
    row_blk = lambda off: pl.BlockSpec((None, tq, W_BR), lambda bi, i: (bi, i, off // W_BR))
    seq_blk = lambda off: pl.BlockSpec((None, seq, W_BR), lambda bi, i: (bi, 0, off // W_BR))
    vt = _value_rows(u[:, :, OFF_CV:OFF_CV + W_BR])
    return pl.pallas_call(
        functools.partial(_na_kernel, rows=rows),
        grid=(b, seq // tq),
        in_specs=[
            row_blk(OFF_CQ), seq_blk(OFF_CK),
            pl.BlockSpec((None,) + vt.shape[1:], lambda bi, i: (bi, 0, 0, 0)),
            row_blk(OFF_CG),
            pl.BlockSpec((None,) + na_tab.shape[1:], lambda bi, i: (layer, 0, 0, 0, 0),
                         pipeline_mode=pl.Buffered(1)),
        ],
        out_specs=pl.BlockSpec((None, tq, W_BR), lambda bi, i: (bi, i, 0)),
        out_shape=jax.ShapeDtypeStruct((b, seq, W_BR), BF16),
        compiler_params=pltpu.CompilerParams(dimension_semantics=("arbitrary", "arbitrary"),
                                             vmem_limit_bytes=VMEM_LIMIT),
        name="nbr_attn",
    )(u, u, vt, u, na_tab)


def _merge_kernel(x_ref, ya_ref, yb_ref, yc_ref, mg_ref, p_ref, wa_ref, wb_ref, wc_ref, wo_ref,
                  wpg_ref, wpe_ref, pg_ref, o_ref):
    def gate(k):
        return jax.nn.sigmoid(mg_ref[:, k * D_MODEL:(k + 1) * D_MODEL].astype(F32))

    m = gate(0) * _dot(ya_ref[...], wa_ref[...])
    m = m + gate(1) * _dot(yb_ref[...], wb_ref[...])
    m = m + gate(2) * _dot(yc_ref[...], wc_ref[...])
    r = _dot(m.astype(BF16), wo_ref[...])
    r = r * lax.rsqrt(jnp.mean(r * r, axis=-1, keepdims=True) + EPS) * pg_ref[...]
    x = x_ref[...] + r
    emb = _dot(p_ref[...].astype(BF16), wpe_ref[...])
    o_ref[...] = x + jax.nn.sigmoid(_dot(x.astype(BF16), wpg_ref[...])) * emb


def _merge(x, ya, yb, yc, u, p, wa, wb, wc, wo, wpg, wpe, post_g, layer, tm):
    n = x.shape[0]
    tok = lambda width: pl.BlockSpec((tm, width), lambda t: (t, 0))
    wgt = lambda r, c: pl.BlockSpec((None, r, c), lambda t: (layer, 0, 0))
    return pl.pallas_call(
        _merge_kernel,
        grid=(n // tm,),
        in_specs=[
            tok(D_MODEL), tok(W_BR), tok(W_BR), tok(W_BR),
            pl.BlockSpec((tm, 3 * D_MODEL), lambda t: (t, OFF_MG // (3 * D_MODEL))),
            pl.BlockSpec((None, tm, D_PLE), lambda t: (layer, t, 0)),
            wgt(W_BR, D_MODEL), wgt(W_BR, D_MODEL), wgt(W_BR, D_MODEL),
            wgt(D_MODEL, D_MODEL), wgt(D_MODEL, D_MODEL), wgt(D_PLE, D_MODEL),
            wgt(1, D_MODEL),
        ],
        out_specs=tok(D_MODEL),
        out_shape=jax.ShapeDtypeStruct((n, D_MODEL), F32),
        compiler_params=pltpu.CompilerParams(dimension_semantics=("arbitrary",),
                                             vmem_limit_bytes=VMEM_LIMIT),
        name="merge",
    )(x, ya, yb, yc, u, p, wa, wb, wc, wo, wpg, wpe, post_g)


def _t5_bucket(rel):
    nb = T5_BUCKETS // 2
    ret = jnp.where(rel > 0, nb, 0)
    n = jnp.abs(rel)
    max_exact = nb // 2
    nf = jnp.maximum(n, 1).astype(F32)
    large = max_exact + (jnp.log(nf / max_exact) / math.log(T5_MAX_DIST / max_exact)
                         * (nb - max_exact)).astype(jnp.int32)
    large = jnp.minimum(large, nb - 1)
    return ret + jnp.where(n < max_exact, n, large)


def _t5_tiles(t5_table, seq, tq, tk):
    assert tk >= T5_MAX_DIST and tq % tk == 0
    rel = jnp.arange(-(seq - 1), seq)
    rel_bias = t5_table[_t5_bucket(rel)].astype(F32).T * LOG2E
    big = tq + (tq // tk + 2) * tk
    padded = jnp.pad(rel_bias, ((0, 0), (big, big)), mode="edge")
    period = tq + tk
    tiles = []
    for e in range(-2, tq // tk + 2):
        base = seq - 1 + big + e * tk
        seg = padded[:, base - tq:base + tk]
        w = jnp.concatenate([seg[:, tq:], seg[:, :tq]], axis=1)
        toe = jnp.tile(w, (1, tq))[:, :tq * (period - 1)].reshape(-1, tq, period - 1)
        tiles.append(toe[:, :, :tk])
    return jnp.swapaxes(jnp.stack(tiles), 2, 3)


def _rope_tables(seq):
    t = jnp.arange(seq)
    row = (t // GRID_W).astype(F32)
    col = (t % GRID_W).astype(F32)
    n_freq = DH // 4
    inv = ROPE_THETA ** (-jnp.arange(n_freq, dtype=F32) / n_freq)
    ang = jnp.concatenate([row[:, None] * inv] * 2 + [col[:, None] * inv] * 2, axis=1)
    cos, sin = jnp.cos(ang), jnp.sin(ang)
    first_half = (jnp.arange(DH) % (2 * n_freq)) < n_freq
    sin_lo = jnp.where(first_half, -sin, 0.0)
    sin_hi = jnp.where(first_half, 0.0, sin)
    two = lambda a: jnp.concatenate([a, a], axis=1)
    return two(cos), two(sin_lo), two(sin_hi)


def _na_tables(rpb, seq):
    rows = seq // GRID_W
    nblk = rows // 2
    assert rows >= NA_BAND and nblk >= 5
    wr = min(NA_ROWS, rows)
    depth = rpb.shape[0]
    ncol = 2 * NA_COLS - 1
    bis = jnp.array([0, 1, 2, nblk - 2, nblk - 1])[:, None, None, None, None]
    qr = 2 * bis + jnp.arange(2)[None, :, None, None, None]
    qc = jnp.arange(GRID_W)[None, None, :, None, None]
    kr = (jnp.clip(2 * bis - NA_ROWS // 2, 0, rows - NA_BAND)
          + jnp.arange(NA_BAND)[None, None, None, :, None])
    kc = jnp.arange(GRID_W)[None, None, None, None, :]
    sr = jnp.clip(qr - wr // 2, 0, rows - wr)
    sc = jnp.clip(qc - NA_COLS // 2, 0, GRID_W - NA_COLS)
    mask = (kr >= sr) & (kr < sr + wr) & (kc >= sc) & (kc < sc + NA_COLS)
    ri = jnp.clip(kr - qr + NA_ROWS - 1, 0, 2 * NA_ROWS - 2)[:, :, 0, :, 0]
    ci = jnp.clip(kc - qc + NA_COLS - 1, 0, ncol - 1)[0, 0, :, 0, :]
    by_row = rpb.astype(F32)[:, :, ri, :]
    onehot = (jnp.arange(ncol)[:, None] == ci.reshape(1, -1)).astype(F32)
    bias = jnp.dot(by_row.reshape(-1, ncol), onehot, precision=lax.Precision.HIGHEST)
    bias = bias.reshape(depth, H_C, 5, 2, NA_BAND, GRID_W, GRID_W)
    bias = jnp.transpose(bias, (0, 2, 1, 3, 5, 4, 6))
    bias = jnp.where(mask[None, :, None], bias * LOG2E, -jnp.inf)
    bias = bias.reshape(depth, 5, H_C // 2, 2, 2, GRID_W, NA_BAND, GRID_W)
    bias = jnp.transpose(bias, (0, 1, 2, 6, 7, 3, 4, 5))
    return bias.reshape(depth, 5, H_C // 2, NA_BAND * GRID_W, 2 * NA_QBLK)


def _prepare(t5_table, pre_norm_g, w_in, subln_g, q_norm_g, k_norm_g, na_rpb, w_branch_a, w_branch_b,
             w_branch_c, w_out, post_norm_g, w_ple_proj, w_ple_gate, seq, tq):
    depth = w_in.shape[0]
    two = lambda a: jnp.concatenate([a, a], axis=-1).reshape(depth, 1, LANES)
    return dict(
        w_in=_permute_in_proj(w_in).astype(BF16),
        pre_g=pre_norm_g.reshape(depth, 1, D_MODEL),
        qn=two(q_norm_g), kn=two(k_norm_g),
        subln=subln_g.reshape(depth, 1, LANES),
        wa=w_branch_a.astype(BF16), wb=_pair_major_heads(w_branch_b, 1).astype(BF16), wc=w_branch_c.astype(BF16),
        wo=w_out.astype(BF16), wpg=w_ple_gate.astype(BF16), wpe=w_ple_proj.astype(BF16),
        post_g=post_norm_g.reshape(depth, 1, D_MODEL),
        t5=_t5_tiles(t5_table, seq, tq, T5_TK), rope=_rope_tables(seq), na=_na_tables(na_rpb, seq),
    )


def _trunk(x, p, lambda_qk, w, *, tm, tq_a, tq_b, tq_c):
    b, seq, _ = x.shape
    depth = p.shape[0]
    n = b * seq
    xf = x.reshape(n, D_MODEL)
    pf = p.reshape(depth, n, D_PLE)
    for layer in range(depth):
        u = _in_proj(xf, w["pre_g"], w["w_in"], w["qn"], w["kn"], w["rope"], layer, seq, tm)
        u3 = u.reshape(b, seq, IN_W)
        ya = _diff_attn(u3, lambda_qk, w["subln"], w["t5"], layer, tq_a)
        yb = _gqa_attn(u3, tq_b)
        yc = _na_attn(u3, w["na"], layer, tq_c)
        xf = _merge(xf, ya.reshape(n, W_BR), yb.reshape(n, W_BR), yc.reshape(n, W_BR), u, pf,
                    w["wa"], w["wb"], w["wc"], w["wo"], w["wpg"], w["wpe"], w["post_g"], layer, tm)
    return xf.reshape(b, seq, D_MODEL)


TILES = dict(tm=512, tq_a=512, tq_b=512, tq_c=512)


def kernel(x_prompt, x_sample, p_prompt, p_sample, t5_table, pre_norm_g, w_in, lambda_qk, subln_g, q_norm_g,
           k_norm_g, na_rpb, w_branch_a, w_branch_b, w_branch_c, w_out, post_norm_g, w_ple_proj, w_ple_gate):
    assert x_prompt.shape[1] == x_sample.shape[1]
    w = _prepare(t5_table, pre_norm_g, w_in, subln_g, q_norm_g, k_norm_g, na_rpb, w_branch_a, w_branch_b,
                 w_branch_c, w_out, post_norm_g, w_ple_proj, w_ple_gate, x_prompt.shape[1], TILES["tq_a"])
    y_prompt = _trunk(x_prompt, p_prompt, lambda_qk, w, **TILES)
    y_sample = _trunk(x_sample, p_sample, lambda_qk, w, **TILES)
    return (y_prompt, y_sample)
```

```python
import functools
import math

import jax
import jax.numpy as jnp
from jax import lax
from jax.experimental import pallas as pl
from jax.experimental.pallas import tpu as pltpu

F32 = jnp.float32
BF16 = jnp.bfloat16

D_MODEL = 1024
D_PLE = 256
EPS = 1e-6
GRID_W = 64
LANES = 128
SUBLANES = 8
ONES_ROWS = 16
DH = 64
H_A = 4
H_B = 8
KV_B = 2
REP_B = H_B // KV_B
H_C = 8
W_BR = 512
NA_ROWS = 8
NA_COLS = 16
NA_BAND = 10
NA_QBLK = 2 * GRID_W
T5_BUCKETS = 32
T5_MAX_DIST = 128
ATTN_TK = 256
T5_TK = ATTN_TK
ROPE_THETA = 10000.0
LOG2E = math.log2(math.e)
QK_SCALE = DH ** -0.5 * LOG2E

OFF_MG = 0
OFF_AQ, OFF_AK, OFF_AG = 3072, 3584, 4096
OFF_BQ, OFF_BG = 4608, 5120
OFF_CQ, OFF_CK, OFF_CG = 5632, 6144, 6656
OFF_BK = 7168
U_W = OFF_BK + LANES
W_AV, W_CV, W_BKV = 7168, 7680, 8192
IN_W = 8448

VMEM_LIMIT = 56 * 1024 * 1024


def _pair_major_heads(w, axis):
    shp = w.shape
    w = w.reshape(shp[:axis] + (KV_B, REP_B, DH) + shp[axis + 1:])
    return jnp.swapaxes(w, axis, axis + 1).reshape(shp)


def _permute_in_proj(w_in):
    sizes = dict(aq=512, ak=512, av=512, ag=512, bq=512, bk=128, bv=128, bg=512, cq=512, ck=512, cv=512,
                 cg=512, mg=3072)
    seg, off = {}, 0
    for name, size in sizes.items():
        seg[name] = w_in[:, :, off:off + size]
        off += size
    seg["aq"] = seg["aq"] * QK_SCALE
    seg["cq"] = seg["cq"] * QK_SCALE
    seg["bq"] = _pair_major_heads(seg["bq"], 2)
    seg["bg"] = _pair_major_heads(seg["bg"], 2)
    order = ("mg", "aq", "ak", "ag", "bq", "bg", "cq", "ck", "cg", "av", "cv", "bk", "bv")
    return jnp.concatenate([seg[n] for n in order], axis=2)


def _silu(x):
    return x * jax.nn.sigmoid(x)


def _dot_nt(a, b):
    return lax.dot_general(a, b, (((1,), (1,)), ((), ())), preferred_element_type=F32)


def _dot(a, b):
    return jnp.dot(a, b, preferred_element_type=F32)


def _pair_norm_rope(a, gain, cos, sin_lo, sin_hi, left):
    sq = a * a
    s_l = jnp.sum(jnp.where(left, sq, 0.0), axis=-1, keepdims=True)
    s_r = jnp.sum(jnp.where(left, 0.0, sq), axis=-1, keepdims=True)
    ms = jnp.where(left, s_l, s_r) * (1.0 / DH)
    y = a * lax.rsqrt(ms + EPS) * gain
    return y * cos + pltpu.roll(y, 112, 1) * sin_lo + pltpu.roll(y, 16, 1) * sin_hi


def _in_proj_kernel(x_ref, g_ref, w_ref, qn_ref, kn_ref, cos_ref, slo_ref, shi_ref,
                    u_ref, vta_ref, vtb_ref, vtc_ref):
    x = x_ref[...]
    ms = jnp.mean(x * x, axis=-1, keepdims=True)
    h = (x * lax.rsqrt(ms + EPS) * g_ref[...]).astype(BF16)
    tm = x.shape[0]
    left = lax.broadcasted_iota(jnp.int32, (tm, LANES), 1) < DH
    cos, slo, shi = cos_ref[...], slo_ref[...], shi_ref[...]
    for c0 in range(0, OFF_BK, W_BR):
        acc = _dot(h, w_ref[:, c0:c0 + W_BR])
        if c0 == OFF_BQ:
            for b in range(W_BR // LANES):
                blk = _pair_norm_rope(acc[:, b * LANES:(b + 1) * LANES], qn_ref[...], cos, slo, shi, left)
                u_ref[:, c0 + b * LANES:c0 + (b + 1) * LANES] = (blk * QK_SCALE).astype(BF16)
        else:
            u_ref[:, c0:c0 + W_BR] = acc.astype(BF16)
    for c0, vt_ref in ((W_AV, vta_ref), (W_CV, vtc_ref)):
        vt = _dot(h, w_ref[:, c0:c0 + W_BR]).T
        vt_ref[...] = vt.reshape(W_BR // LANES, LANES, tm).astype(BF16)
    acc = _dot(h, w_ref[:, W_BKV:IN_W])
    bk = _pair_norm_rope(acc[:, :LANES], kn_ref[...], cos, slo, shi, left)
    u_ref[:, OFF_BK:U_W] = bk.astype(BF16)
    vtb_ref[0] = acc[:, LANES:].T.astype(BF16)


def _in_proj(x, pre_g, w_in, qn, kn, rope, layer, seq, tm):
    n = x.shape[0]
    nseq = seq // tm
    cos, slo, shi = rope
    vec = lambda width: pl.BlockSpec((None, 1, width), lambda t: (layer, 0, 0))
    tab = pl.BlockSpec((tm, LANES), lambda t: (t % nseq, 0))
    vt_spec = lambda heads: pl.BlockSpec((None, heads, LANES, tm), lambda t: (t // nseq, 0, 0, t % nseq))
    vt_shape = lambda heads: jax.ShapeDtypeStruct((n // seq, heads, LANES, seq), BF16)
    return pl.pallas_call(
        _in_proj_kernel,
        grid=(n // tm,),
        in_specs=[
            pl.BlockSpec((tm, D_MODEL), lambda t: (t, 0)),
            vec(D_MODEL),
            pl.BlockSpec((None, D_MODEL, IN_W), lambda t: (layer, 0, 0), pipeline_mode=pl.Buffered(1)),
            vec(LANES), vec(LANES), tab, tab, tab,
        ],
        out_specs=[pl.BlockSpec((tm, U_W), lambda t: (t, 0)), vt_spec(H_A), vt_spec(1), vt_spec(H_C // 2)],
        out_shape=[jax.ShapeDtypeStruct((n, U_W), BF16), vt_shape(H_A), vt_shape(1), vt_shape(H_C // 2)],
        compiler_params=pltpu.CompilerParams(dimension_semantics=("arbitrary",),
                                             vmem_limit_bytes=VMEM_LIMIT),
        name="in_proj",
    )(x, pre_g, w_in, qn, kn, cos, slo, shi)


def _two_stage_scratch(tq, seq, tk):
    return [pltpu.VMEM((2, seq // tk, tk, tq), F32), pltpu.VMEM((2, SUBLANES, tq), F32)] * 2


def _two_stage_step(qs, k_ref, vt_ref, bias, s_w, m_w, s_r, m_r):
    _, nchunk, tk, tq = s_w.shape
    slabs = tk // SUBLANES
    m_run, acc = [None] * 2, [None] * 2
    for c in range(nchunk):
        kc = k_ref[c * tk:(c + 1) * tk, :]
        vtc = _with_ones(vt_ref[:, c * tk:(c + 1) * tk])
        for e in range(2):
            s = _dot_nt(kc, qs[e])
            if bias is not None:
                s = s + bias(c, e)
            s_w[e, c] = s
            m_c = jnp.max(s.reshape(slabs, SUBLANES, tq), axis=0)
            m_run[e] = m_c if c == 0 else jnp.maximum(m_run[e], m_c)
            p = jnp.exp2(s_r[e, c].reshape(slabs, SUBLANES, tq) - m_r[e][None])
            pv = _dot(vtc, p.reshape(tk, tq).astype(BF16))
            acc[e] = pv if c == 0 else acc[e] + pv
    outs = []
    for e in range(2):
        m_w[e] = jnp.broadcast_to(jnp.max(m_run[e], axis=0, keepdims=True), (SUBLANES, tq))
        outs.append((acc[e][:LANES] / acc[e][LANES:LANES + 1]).T)
    return outs


def _two_stage(step, scratch):
    s_a, m_a, s_b, m_b = scratch
    t = pl.program_id(0)

    @pl.when(t == 0)
    def _():
        s_b[...] = jnp.zeros_like(s_b)
        m_b[...] = jnp.zeros_like(m_b)

    @pl.when(t % 2 == 0)
    def _():
        step(s_a, m_a, s_b, m_b)

    @pl.when(t % 2 == 1)
    def _():
        step(s_b, m_b, s_a, m_a)


def _lane_halves(q):
    left = lax.broadcasted_iota(jnp.int32, q.shape, 1) < DH
    zero = jnp.zeros_like(q)
    return jnp.where(left, q, zero), jnp.where(left, zero, q)


def _unit_maps(b, heads, n_i):
    units = b * heads * n_i

    def unit(t, lag):
        un = jnp.clip(t - lag, 0, units - 1)
        return un // (heads * n_i), (un // n_i) % heads, un % n_i

    return units, unit


def _diff_attn_kernel(lam_ref, sg_ref, q_ref, k_ref, vt_ref, g_ref, t_ref, o_ref, *scratch,
                      lambda_init, n_i, units):
    i = jnp.minimum(pl.program_id(0), units - 1) % n_i
    per_q = q_ref.shape[0] // t_ref.shape[2]

    def bias(c, mp):
        return t_ref[jnp.clip(c - i * per_q, -2, per_q + 1) + 2, mp]

    def step(*bufs):
        lp = lam_ref[...]
        lam = (jnp.exp(jnp.sum(lp[0:1] * lp[1:2], keepdims=True))
               - jnp.exp(jnp.sum(lp[2:3] * lp[3:4], keepdims=True)) + lambda_init)
        o0, o1 = _two_stage_step(_lane_halves(q_ref[...]), k_ref, vt_ref, bias, *bufs)
        o = o0 - lam * o1
        o = o * lax.rsqrt(jnp.mean(o * o, axis=-1, keepdims=True) + EPS) * sg_ref[...]
        o = o * (1.0 - lambda_init)
        o_ref[...] = (o * _silu(g_ref[...].astype(F32))).astype(BF16)

    _two_stage(step, scratch)


def _with_ones(vt):
    return jnp.concatenate([vt, jnp.ones((ONES_ROWS, vt.shape[1]), vt.dtype)], axis=0)


def _diff_attn(u, vt, lam_qk, subln_g, t5_tiles, layer, tq):
    b, seq, _ = u.shape
    ntile, _, tk, _ = t5_tiles.shape
    n_i = seq // tq
    units, unit = _unit_maps(b, H_A, n_i)
    lambda_init = 0.8 - 0.6 * math.exp(-0.3 * layer)

    def row_blk(off, lag):
        def index(t):
            bi, h, i = unit(t, lag)
            return bi, i, off // LANES + h
        return pl.BlockSpec((None, tq, LANES), index)

    def seq_blk(off, lag):
        def index(t):
            bi, h, _ = unit(t, lag)
            return bi, 0, off // LANES + h
        return pl.BlockSpec((None, seq, LANES), index)

    return pl.pallas_call(
        functools.partial(_diff_attn_kernel, lambda_init=lambda_init, n_i=n_i, units=units),
        grid=(units + 1,),
        in_specs=[
            pl.BlockSpec((None, 4, DH), lambda t: (layer, 0, 0)),
            pl.BlockSpec((None, 1, LANES), lambda t: (layer, 0, 0)),
            row_blk(OFF_AQ, 0), seq_blk(OFF_AK, 0),
            pl.BlockSpec((None, None, LANES, seq), lambda t: unit(t, 1)[:2] + (0, 0)),
            row_blk(OFF_AG, 1),
            pl.BlockSpec((ntile, 2, tk, tq), lambda t: (0, unit(t, 0)[1], 0, 0)),
        ],
        out_specs=row_blk(0, 1),
        out_shape=jax.ShapeDtypeStruct((b, seq, W_BR), BF16),
        scratch_shapes=_two_stage_scratch(tq, seq, tk),
        compiler_params=pltpu.CompilerParams(dimension_semantics=("arbitrary",),
                                             vmem_limit_bytes=VMEM_LIMIT),
        name="diff_attn",
    )(lam_qk, subln_g, u, u, vt, u, t5_tiles)


def _gqa_kernel(q_ref, k_ref, vt_ref, g_ref, o_ref, *scratch):
    def step(*bufs):
        o0, o1 = _two_stage_step(_lane_halves(q_ref[...]), k_ref, vt_ref, None, *bufs)
        left = lax.broadcasted_iota(jnp.int32, o0.shape, 1) < DH
        o = jnp.where(left, o0, o1)
        o_ref[...] = (o * _silu(g_ref[...].astype(F32))).astype(BF16)

    _two_stage(step, scratch)


def _gqa_attn(u, vt, tq):
    b, seq, _ = u.shape
    units, unit = _unit_maps(b, REP_B, seq // tq)

    def row_blk(off, lag):
        def index(t):
            bi, pr, i = unit(t, lag)
            return bi, i, off // LANES + pr
        return pl.BlockSpec((None, tq, LANES), index)

    def seq_blk(off, lag):
        return pl.BlockSpec((None, seq, LANES), lambda t: (unit(t, lag)[0], 0, off // LANES))

    return pl.pallas_call(
        _gqa_kernel,
        grid=(units + 1,),
        in_specs=[
            row_blk(OFF_BQ, 0), seq_blk(OFF_BK, 0),
            pl.BlockSpec((None, None, LANES, seq), lambda t: (unit(t, 1)[0], 0, 0, 0)),
            row_blk(OFF_BG, 1),
        ],
        out_specs=row_blk(0, 1),
        out_shape=jax.ShapeDtypeStruct((b, seq, W_BR), BF16),
        scratch_shapes=_two_stage_scratch(tq, seq, ATTN_TK),
        compiler_params=pltpu.CompilerParams(dimension_semantics=("arbitrary",),
                                             vmem_limit_bytes=VMEM_LIMIT),
        name="gqa_attn",
    )(u, u, vt, u)


def _na_block_type(bi, nblk):
    return jnp.where(bi < 2, bi, jnp.where(bi <= nblk - 3, 2, bi - (nblk - 5)))


def _na_kernel(q_ref, k_ref, vt_ref, g_ref, tab_ref, o_ref, *, rows):
    big = pl.program_id(1)
    nsub = q_ref.shape[0] // NA_QBLK
    nblk = rows // 2
    nkeys = NA_BAND * GRID_W
    left = lax.broadcasted_iota(jnp.int32, (NA_QBLK, LANES), 1) < DH

    def slices(sub, pair):
        bi = big * nsub + sub
        b0 = jnp.clip(2 * bi - NA_ROWS // 2, 0, rows - NA_BAND)
        ks = pl.ds(pl.multiple_of(b0 * GRID_W, 2 * GRID_W), nkeys)
        return (_na_block_type(bi, nblk), ks, slice(sub * NA_QBLK, (sub + 1) * NA_QBLK),
                slice(pair * LANES, (pair + 1) * LANES))

    def scores(sub, pair):
        typ, ks, rs, cs = slices(sub, pair)
        qq = jnp.concatenate(_lane_halves(q_ref[rs, cs]), axis=0)
        return _dot_nt(k_ref[ks, cs], qq) + tab_ref[typ, pair]

    def weights(s):
        return jnp.exp2(s - jnp.max(s, axis=0, keepdims=True)).astype(BF16)

    def finish(sub, pair, p):
        _, ks, rs, cs = slices(sub, pair)
        ot = _dot(_with_ones(vt_ref[pair, :, ks]), p)
        o = (ot[:LANES] / ot[LANES:LANES + 1]).T
        o = jnp.where(left, o[:NA_QBLK], o[NA_QBLK:])
        o_ref[rs, cs] = (o * _silu(g_ref[rs, cs].astype(F32))).astype(BF16)

    units = [(sub, pair) for sub in range(nsub) for pair in range(H_C // 2)]
    s_prev, p_prev = None, None
    for n in range(len(units) + 2):
        s_new = scores(*units[n]) if n < len(units) else None
        p_new = weights(s_prev) if s_prev is not None else None
        if p_prev is not None:
            finish(*units[n - 2], p_prev)
        s_prev, p_prev = s_new, p_new


def _na_attn(u, vt, na_tab, layer, tq):
    b, seq, _ = u.shape
    rows = seq // GRID_W
    assert (rows - NA_BAND) % 2 == 0
    row_blk = lambda off: pl.BlockSpec((None, tq, W_BR), lambda bi, i: (bi, i, off // W_BR))
    seq_blk = lambda off: pl.BlockSpec((None, seq, W_BR), lambda bi, i: (bi, 0, off // W_BR))
    return pl.pallas_call(
        functools.partial(_na_kernel, rows=rows),
        grid=(b, seq // tq),
        in_specs=[
            row_blk(OFF_CQ), seq_blk(OFF_CK),
            pl.BlockSpec((None,) + vt.shape[1:], lambda bi, i: (bi, 0, 0, 0)),
            row_blk(OFF_CG),
            pl.BlockSpec((None,) + na_tab.shape[1:], lambda bi, i: (layer, 0, 0, 0, 0),
                         pipeline_mode=pl.Buffered(1)),
        ],
        out_specs=pl.BlockSpec((None, tq, W_BR), lambda bi, i: (bi, i, 0)),
        out_shape=jax.ShapeDtypeStruct((b, seq, W_BR), BF16),
        compiler_params=pltpu.CompilerParams(dimension_semantics=("arbitrary", "arbitrary"),
                                             vmem_limit_bytes=VMEM_LIMIT),
        name="nbr_attn",
    )(u, u, vt, u, na_tab)


def _merge_kernel(x_ref, ya_ref, yb_ref, yc_ref, mg_ref, p_ref, wa_ref, wb_ref, wc_ref, wo_ref,
                  wpg_ref, wpe_ref, pg_ref, o_ref):
    def gate(k):
        return jax.nn.sigmoid(mg_ref[:, k * D_MODEL:(k + 1) * D_MODEL].astype(F32))

    m = gate(0) * _dot(ya_ref[...], wa_ref[...])
    m = m + gate(1) * _dot(yb_ref[...], wb_ref[...])
    m = m + gate(2) * _dot(yc_ref[...], wc_ref[...])
    r = _dot(m.astype(BF16), wo_ref[...])
    r = r * lax.rsqrt(jnp.mean(r * r, axis=-1, keepdims=True) + EPS) * pg_ref[...]
    x = x_ref[...] + r
    emb = _dot(p_ref[...].astype(BF16), wpe_ref[...])
    o_ref[...] = x + jax.nn.sigmoid(_dot(x.astype(BF16), wpg_ref[...])) * emb


def _merge(x, ya, yb, yc, u, p, wa, wb, wc, wo, wpg, wpe, post_g, layer, tm):
    n = x.shape[0]
    tok = lambda width: pl.BlockSpec((tm, width), lambda t: (t, 0))
    wgt = lambda r, c: pl.BlockSpec((None, r, c), lambda t: (layer, 0, 0))
    return pl.pallas_call(
        _merge_kernel,
        grid=(n // tm,),
        in_specs=[
            tok(D_MODEL), tok(W_BR), tok(W_BR), tok(W_BR),
            pl.BlockSpec((tm, 3 * D_MODEL), lambda t: (t, OFF_MG // (3 * D_MODEL))),
            pl.BlockSpec((None, tm, D_PLE), lambda t: (layer, t, 0)),
            wgt(W_BR, D_MODEL), wgt(W_BR, D_MODEL), wgt(W_BR, D_MODEL),
            wgt(D_MODEL, D_MODEL), wgt(D_MODEL, D_MODEL), wgt(D_PLE, D_MODEL),
            wgt(1, D_MODEL),
        ],
        out_specs=tok(D_MODEL),
        out_shape=jax.ShapeDtypeStruct((n, D_MODEL), F32),
        compiler_params=pltpu.CompilerParams(dimension_semantics=("arbitrary",),
                                             vmem_limit_bytes=VMEM_LIMIT),
        name="merge",
    )(x, ya, yb, yc, u, p, wa, wb, wc, wo, wpg, wpe, post_g)


def _t5_bucket(rel):
    nb = T5_BUCKETS // 2
    ret = jnp.where(rel > 0, nb, 0)
    n = jnp.abs(rel)
    max_exact = nb // 2
    nf = jnp.maximum(n, 1).astype(F32)
    large = max_exact + (jnp.log(nf / max_exact) / math.log(T5_MAX_DIST / max_exact)
                         * (nb - max_exact)).astype(jnp.int32)
    large = jnp.minimum(large, nb - 1)
    return ret + jnp.where(n < max_exact, n, large)


def _t5_tiles(t5_table, seq, tq, tk):
    assert tk >= T5_MAX_DIST and tq % tk == 0
    rel = jnp.arange(-(seq - 1), seq)
    rel_bias = t5_table[_t5_bucket(rel)].astype(F32).T * LOG2E
    big = tq + (tq // tk + 2) * tk
    padded = jnp.pad(rel_bias, ((0, 0), (big, big)), mode="edge")
    period = tq + tk
    tiles = []
    for e in range(-2, tq // tk + 2):
        base = seq - 1 + big + e * tk
        seg = padded[:, base - tq:base + tk]
        w = jnp.concatenate([seg[:, tq:], seg[:, :tq]], axis=1)
        toe = jnp.tile(w, (1, tq))[:, :tq * (period - 1)].reshape(-1, tq, period - 1)
        tiles.append(toe[:, :, :tk])
    return jnp.swapaxes(jnp.stack(tiles), 2, 3)


def _rope_tables(seq):
    t = jnp.arange(seq)
    row = (t // GRID_W).astype(F32)
    col = (t % GRID_W).astype(F32)
    n_freq = DH // 4
    inv = ROPE_THETA ** (-jnp.arange(n_freq, dtype=F32) / n_freq)
    ang = jnp.concatenate([row[:, None] * inv] * 2 + [col[:, None] * inv] * 2, axis=1)
    cos, sin = jnp.cos(ang), jnp.sin(ang)
    first_half = (jnp.arange(DH) % (2 * n_freq)) < n_freq
    sin_lo = jnp.where(first_half, -sin, 0.0)
    sin_hi = jnp.where(first_half, 0.0, sin)
    two = lambda a: jnp.concatenate([a, a], axis=1)
    return two(cos), two(sin_lo), two(sin_hi)


def _na_tables(rpb, seq):
    rows = seq // GRID_W
    nblk = rows // 2
    assert rows >= NA_BAND and nblk >= 5
    wr = min(NA_ROWS, rows)
    depth = rpb.shape[0]
    ncol = 2 * NA_COLS - 1
    bis = jnp.array([0, 1, 2, nblk - 2, nblk - 1])[:, None, None, None, None]
    qr = 2 * bis + jnp.arange(2)[None, :, None, None, None]
    qc = jnp.arange(GRID_W)[None, None, :, None, None]
    kr = (jnp.clip(2 * bis - NA_ROWS // 2, 0, rows - NA_BAND)
          + jnp.arange(NA_BAND)[None, None, None, :, None])
    kc = jnp.arange(GRID_W)[None, None, None, None, :]
    sr = jnp.clip(qr - wr // 2, 0, rows - wr)
    sc = jnp.clip(qc - NA_COLS // 2, 0, GRID_W - NA_COLS)
    mask = (kr >= sr) & (kr < sr + wr) & (kc >= sc) & (kc < sc + NA_COLS)
    ri = jnp.clip(kr - qr + NA_ROWS - 1, 0, 2 * NA_ROWS - 2)[:, :, 0, :, 0]
    ci = jnp.clip(kc - qc + NA_COLS - 1, 0, ncol - 1)[0, 0, :, 0, :]
    by_row = rpb.astype(F32)[:, :, ri, :]
    by_row = by_row.reshape(depth, H_C // 2, 2, 5, 2, NA_BAND, ncol)
    by_row = jnp.transpose(by_row, (0, 3, 1, 5, 2, 4, 6))
    onehot = (jnp.arange(ncol)[:, None] == ci.T.reshape(1, -1)).astype(F32)
    bias = jnp.dot(by_row.reshape(-1, ncol), onehot, precision=lax.Precision.HIGHEST)
    bias = bias.reshape(depth, 5, H_C // 2, NA_BAND, 2, 2, GRID_W, GRID_W)
    bias = jnp.transpose(bias, (0, 1, 2, 3, 6, 4, 5, 7))
    mask = jnp.transpose(mask, (0, 3, 4, 1, 2))
    bias = jnp.where(mask[None, :, None, :, :, None], bias * LOG2E, -jnp.inf)
    return bias.reshape(depth, 5, H_C // 2, NA_BAND * GRID_W, 2 * NA_QBLK)


def _prepare(t5_table, pre_norm_g, w_in, subln_g, q_norm_g, k_norm_g, na_rpb, w_branch_a, w_branch_b,
             w_branch_c, w_out, post_norm_g, w_ple_proj, w_ple_gate, seq, tq):
    depth = w_in.shape[0]
    two = lambda a: jnp.concatenate([a, a], axis=-1).reshape(depth, 1, LANES)
    return dict(
        w_in=_permute_in_proj(w_in).astype(BF16),
        pre_g=pre_norm_g.reshape(depth, 1, D_MODEL),
        qn=two(q_norm_g), kn=two(k_norm_g),
        subln=subln_g.reshape(depth, 1, LANES),
        wa=w_branch_a.astype(BF16), wb=_pair_major_heads(w_branch_b, 1).astype(BF16), wc=w_branch_c.astype(BF16),
        wo=w_out.astype(BF16), wpg=w_ple_gate.astype(BF16), wpe=w_ple_proj.astype(BF16),
        post_g=post_norm_g.reshape(depth, 1, D_MODEL),
        t5=_t5_tiles(t5_table, seq, tq, T5_TK), rope=_rope_tables(seq), na=_na_tables(na_rpb, seq),
    )


def _trunk(x, p, lambda_qk, w, *, tm, tq_a, tq_b, tq_c):
    b, seq, _ = x.shape
    depth = p.shape[0]
    n = b * seq
    xf = x.reshape(n, D_MODEL)
    pf = p.reshape(depth, n, D_PLE)
    for layer in range(depth):
        u, vta, vtb, vtc = _in_proj(xf, w["pre_g"], w["w_in"], w["qn"], w["kn"], w["rope"], layer, seq, tm)
        u3 = u.reshape(b, seq, U_W)
        ya = _diff_attn(u3, vta, lambda_qk, w["subln"], w["t5"], layer, tq_a)
        yb = _gqa_attn(u3, vtb, tq_b)
        yc = _na_attn(u3, vtc, w["na"], layer, tq_c)
        xf = _merge(xf, ya.reshape(n, W_BR), yb.reshape(n, W_BR), yc.reshape(n, W_BR), u, pf,
                    w["wa"], w["wb"], w["wc"], w["wo"], w["wpg"], w["wpe"], w["post_g"], layer, tm)
    return xf.reshape(b, seq, D_MODEL)


TILES = dict(tm=512, tq_a=512, tq_b=512, tq_c=512)


def kernel(x_prompt, x_sample, p_prompt, p_sample, t5_table, pre_norm_g, w_in, lambda_qk, subln_g, q_norm_g,
           k_norm_g, na_rpb, w_branch_a, w_branch_b, w_branch_c, w_out, post_norm_g, w_ple_proj, w_ple_gate):
    assert x_prompt.shape[1] == x_sample.shape[1]
    w = _prepare(t5_table, pre_norm_g, w_in, subln_g, q_norm_g, k_norm_g, na_rpb, w_branch_a, w_branch_b,
                 w_branch_c, w_out, post_norm_g, w_ple_proj, w_ple_gate, x_prompt.shape[1], TILES["tq_a"])
    y_prompt = _trunk(x_prompt, p_prompt, lambda_qk, w, **TILES)
    y_sample = _trunk(x_sample, p_sample, lambda_qk, w, **TILES)
    return (y_prompt, y_sample)
```

```python
import functools
import math

import jax
import jax.numpy as jnp
from jax import lax
from jax.experimental import pallas as pl
from jax.experimental.pallas import tpu as pltpu

F32 = jnp.float32
BF16 = jnp.bfloat16

D_MODEL = 1024
D_PLE = 256
EPS = 1e-6
GRID_W = 64
LANES = 128
SUBLANES = 8
ONES_ROWS = 16
DH = 64
H_A = 4
H_B = 8
KV_B = 2
REP_B = H_B // KV_B
H_C = 8
W_BR = 512
NA_ROWS = 8
NA_COLS = 16
NA_BAND = 10
NA_QBLK = 2 * GRID_W
T5_BUCKETS = 32
T5_MAX_DIST = 128
ATTN_TK = 256
T5_TK = ATTN_TK
ROPE_THETA = 10000.0
LOG2E = math.log2(math.e)
QK_SCALE = DH ** -0.5 * LOG2E

OFF_MG = 0
OFF_AQ, OFF_AK, OFF_AG = 3072, 3584, 4096
OFF_BQ, OFF_BG = 4608, 5120
OFF_CQ, OFF_CK, OFF_CG = 5632, 6144, 6656
OFF_BK = 7168
U_W = OFF_BK + LANES
W_AV, W_CV, W_BKV = 7168, 7680, 8192
IN_W = 8448

VMEM_LIMIT = 56 * 1024 * 1024


def _pair_major_heads(w, axis):
    shp = w.shape
    w = w.reshape(shp[:axis] + (KV_B, REP_B, DH) + shp[axis + 1:])
    return jnp.swapaxes(w, axis, axis + 1).reshape(shp)


def _permute_in_proj(w_in):
    sizes = dict(aq=512, ak=512, av=512, ag=512, bq=512, bk=128, bv=128, bg=512, cq=512, ck=512, cv=512,
                 cg=512, mg=3072)
    seg, off = {}, 0
    for name, size in sizes.items():
        seg[name] = w_in[:, :, off:off + size]
        off += size
    seg["aq"] = seg["aq"] * QK_SCALE
    seg["cq"] = seg["cq"] * QK_SCALE
    seg["bq"] = _pair_major_heads(seg["bq"], 2)
    seg["bg"] = _pair_major_heads(seg["bg"], 2)
    order = ("mg", "aq", "ak", "ag", "bq", "bg", "cq", "ck", "cg", "av", "cv", "bk", "bv")
    return jnp.concatenate([seg[n] for n in order], axis=2)


def _silu(x):
    return x * jax.nn.sigmoid(x)


def _dot_nt(a, b):
    return lax.dot_general(a, b, (((1,), (1,)), ((), ())), preferred_element_type=F32)


def _dot(a, b):
    return jnp.dot(a, b, preferred_element_type=F32)


def _pair_norm_rope(a, gain, cos, sin_lo, sin_hi, left):
    sq = a * a
    s_l = jnp.sum(jnp.where(left, sq, 0.0), axis=-1, keepdims=True)
    s_r = jnp.sum(jnp.where(left, 0.0, sq), axis=-1, keepdims=True)
    ms = jnp.where(left, s_l, s_r) * (1.0 / DH)
    y = a * lax.rsqrt(ms + EPS) * gain
    return y * cos + pltpu.roll(y, 112, 1) * sin_lo + pltpu.roll(y, 16, 1) * sin_hi


def _in_proj_kernel(x_ref, g_ref, w_ref, qn_ref, kn_ref, cos_ref, slo_ref, shi_ref,
                    u_ref, vta_ref, vtb_ref, vtc_ref):
    x = x_ref[...]
    ms = jnp.mean(x * x, axis=-1, keepdims=True)
    h = (x * lax.rsqrt(ms + EPS) * g_ref[...]).astype(BF16)
    tm = x.shape[0]
    left = lax.broadcasted_iota(jnp.int32, (tm, LANES), 1) < DH
    cos, slo, shi = cos_ref[...], slo_ref[...], shi_ref[...]
    acc = _dot(h, w_ref[:, W_BKV:IN_W])
    bk = _pair_norm_rope(acc[:, :LANES], kn_ref[...], cos, slo, shi, left)
    u_ref[:, OFF_BK:U_W] = bk.astype(BF16)
    vtb_ref[0] = acc[:, LANES:].T.astype(BF16)
    acc = _dot(h, w_ref[:, OFF_BQ:OFF_BQ + W_BR])
    for b in range(W_BR // LANES):
        blk = _pair_norm_rope(acc[:, b * LANES:(b + 1) * LANES], qn_ref[...], cos, slo, shi, left)
        u_ref[:, OFF_BQ + b * LANES:OFF_BQ + (b + 1) * LANES] = (blk * QK_SCALE).astype(BF16)
    for c0, vt_ref in ((W_AV, vta_ref), (W_CV, vtc_ref)):
        vt = _dot(h, w_ref[:, c0:c0 + W_BR]).T
        vt_ref[...] = vt.reshape(W_BR // LANES, LANES, tm).astype(BF16)
    for c0 in range(0, OFF_BK, W_BR):
        if c0 != OFF_BQ:
            u_ref[:, c0:c0 + W_BR] = _dot(h, w_ref[:, c0:c0 + W_BR]).astype(BF16)


def _in_proj(x, pre_g, w_in, qn, kn, rope, layer, seq, tm):
    n = x.shape[0]
    nseq = seq // tm
    cos, slo, shi = rope
    vec = lambda width: pl.BlockSpec((None, 1, width), lambda t: (layer, 0, 0))
    tab = pl.BlockSpec((tm, LANES), lambda t: (t % nseq, 0))
    vt_spec = lambda heads: pl.BlockSpec((None, heads, LANES, tm), lambda t: (t // nseq, 0, 0, t % nseq))
    vt_shape = lambda heads: jax.ShapeDtypeStruct((n // seq, heads, LANES, seq), BF16)
    return pl.pallas_call(
        _in_proj_kernel,
        grid=(n // tm,),
        in_specs=[
            pl.BlockSpec((tm, D_MODEL), lambda t: (t, 0)),
            vec(D_MODEL),
            pl.BlockSpec((None, D_MODEL, IN_W), lambda t: (layer, 0, 0), pipeline_mode=pl.Buffered(1)),
            vec(LANES), vec(LANES), tab, tab, tab,
        ],
        out_specs=[pl.BlockSpec((tm, U_W), lambda t: (t, 0)), vt_spec(H_A), vt_spec(1), vt_spec(H_C // 2)],
        out_shape=[jax.ShapeDtypeStruct((n, U_W), BF16), vt_shape(H_A), vt_shape(1), vt_shape(H_C // 2)],
        compiler_params=pltpu.CompilerParams(dimension_semantics=("arbitrary",),
                                             vmem_limit_bytes=VMEM_LIMIT),
        name="in_proj",
    )(x, pre_g, w_in, qn, kn, cos, slo, shi)


def _two_stage_scratch(tq, seq, tk):
    return [pltpu.VMEM((2, seq // tk, tk, tq), F32), pltpu.VMEM((2, SUBLANES, tq), F32)] * 2


def _two_stage_step(qs, k_ref, vt_ref, bias, s_w, m_w, s_r, m_r):
    _, nchunk, tk, tq = s_w.shape
    slabs = tk // SUBLANES
    m_run, acc = [None] * 2, [None] * 2
    for c in range(nchunk):
        kc = k_ref[c * tk:(c + 1) * tk, :]
        vtc = _with_ones(vt_ref[:, c * tk:(c + 1) * tk])
        for e in range(2):
            p = jnp.exp2(s_r[e, c].reshape(slabs, SUBLANES, tq) - m_r[e][None])
            pv = _dot(vtc, p.reshape(tk, tq).astype(BF16))
            acc[e] = pv if c == 0 else acc[e] + pv
            s = _dot_nt(kc, qs[e])
            if bias is not None:
                s = s + bias(c, e)
            s_w[e, c] = s
            m_c = jnp.max(s.reshape(slabs, SUBLANES, tq), axis=0)
            m_run[e] = m_c if c == 0 else jnp.maximum(m_run[e], m_c)
    outs = []
    for e in range(2):
        m_w[e] = jnp.broadcast_to(jnp.max(m_run[e], axis=0, keepdims=True), (SUBLANES, tq))
        outs.append((acc[e][:LANES] / acc[e][LANES:LANES + 1]).T)
    return outs


def _two_stage(step, scratch):
    s_a, m_a, s_b, m_b = scratch
    t = pl.program_id(0)

    @pl.when(t == 0)
    def _():
        s_b[...] = jnp.zeros_like(s_b)
        m_b[...] = jnp.zeros_like(m_b)

    @pl.when(t % 2 == 0)
    def _():
        step(s_a, m_a, s_b, m_b)

    @pl.when(t % 2 == 1)
    def _():
        step(s_b, m_b, s_a, m_a)


def _lane_halves(q):
    left = lax.broadcasted_iota(jnp.int32, q.shape, 1) < DH
    zero = jnp.zeros_like(q)
    return jnp.where(left, q, zero), jnp.where(left, zero, q)


def _unit_maps(b, heads, n_i):
    units = b * heads * n_i

    def unit(t, lag):
        un = jnp.clip(t - lag, 0, units - 1)
        return un // (heads * n_i), (un // n_i) % heads, un % n_i

    return units, unit


def _diff_attn_kernel(lam_ref, sg_ref, q_ref, k_ref, vt_ref, g_ref, t_ref, o_ref, *scratch,
                      lambda_init, n_i, units):
    i = jnp.minimum(pl.program_id(0), units - 1) % n_i
    per_q = q_ref.shape[0] // t_ref.shape[2]

    def bias(c, mp):
        return t_ref[jnp.clip(c - i * per_q, -2, per_q + 1) + 2, mp]

    def step(*bufs):
        lp = lam_ref[...]
        lam = (jnp.exp(jnp.sum(lp[0:1] * lp[1:2], keepdims=True))
               - jnp.exp(jnp.sum(lp[2:3] * lp[3:4], keepdims=True)) + lambda_init)
        o0, o1 = _two_stage_step(_lane_halves(q_ref[...]), k_ref, vt_ref, bias, *bufs)
        o = o0 - lam * o1
        o = o * lax.rsqrt(jnp.mean(o * o, axis=-1, keepdims=True) + EPS) * sg_ref[...]
        o = o * (1.0 - lambda_init)
        o_ref[...] = (o * _silu(g_ref[...].astype(F32))).astype(BF16)

    _two_stage(step, scratch)


def _with_ones(vt):
    return jnp.concatenate([vt, jnp.ones((ONES_ROWS, vt.shape[1]), vt.dtype)], axis=0)


def _diff_attn(u, vt, lam_qk, subln_g, t5_tiles, layer, tq):
    b, seq, _ = u.shape
    ntile, _, tk, _ = t5_tiles.shape
    n_i = seq // tq
    units, unit = _unit_maps(b, H_A, n_i)
    lambda_init = 0.8 - 0.6 * math.exp(-0.3 * layer)

    def row_blk(off, lag):
        def index(t):
            bi, h, i = unit(t, lag)
            return bi, i, off // LANES + h
        return pl.BlockSpec((None, tq, LANES), index)

    def seq_blk(off, lag):
        def index(t):
            bi, h, _ = unit(t, lag)
            return bi, 0, off // LANES + h
        return pl.BlockSpec((None, seq, LANES), index)

    return pl.pallas_call(
        functools.partial(_diff_attn_kernel, lambda_init=lambda_init, n_i=n_i, units=units),
        grid=(units + 1,),
        in_specs=[
            pl.BlockSpec((None, 4, DH), lambda t: (layer, 0, 0)),
            pl.BlockSpec((None, 1, LANES), lambda t: (layer, 0, 0)),
            row_blk(OFF_AQ, 0), seq_blk(OFF_AK, 0),
            pl.BlockSpec((None, None, LANES, seq), lambda t: unit(t, 1)[:2] + (0, 0)),
            row_blk(OFF_AG, 1),
            pl.BlockSpec((ntile, 2, tk, tq), lambda t: (0, unit(t, 0)[1], 0, 0)),
        ],
        out_specs=row_blk(0, 1),
        out_shape=jax.ShapeDtypeStruct((b, seq, W_BR), BF16),
        scratch_shapes=_two_stage_scratch(tq, seq, tk),
        compiler_params=pltpu.CompilerParams(dimension_semantics=("arbitrary",),
                                             vmem_limit_bytes=VMEM_LIMIT),
        name="diff_attn",
    )(lam_qk, subln_g, u, u, vt, u, t5_tiles)


def _gqa_kernel(q_ref, k_ref, vt_ref, g_ref, o_ref, *scratch):
    def step(*bufs):
        o0, o1 = _two_stage_step(_lane_halves(q_ref[...]), k_ref, vt_ref, None, *bufs)
        left = lax.broadcasted_iota(jnp.int32, o0.shape, 1) < DH
        o = jnp.where(left, o0, o1)
        o_ref[...] = (o * _silu(g_ref[...].astype(F32))).astype(BF16)

    _two_stage(step, scratch)


def _gqa_attn(u, vt, tq):
    b, seq, _ = u.shape
    units, unit = _unit_maps(b, REP_B, seq // tq)

    def row_blk(off, lag):
        def index(t):
            bi, pr, i = unit(t, lag)
            return bi, i, off // LANES + pr
        return pl.BlockSpec((None, tq, LANES), index)

    def seq_blk(off, lag):
        return pl.BlockSpec((None, seq, LANES), lambda t: (unit(t, lag)[0], 0, off // LANES))

    return pl.pallas_call(
        _gqa_kernel,
        grid=(units + 1,),
        in_specs=[
            row_blk(OFF_BQ, 0), seq_blk(OFF_BK, 0),
            pl.BlockSpec((None, None, LANES, seq), lambda t: (unit(t, 1)[0], 0, 0, 0)),
            row_blk(OFF_BG, 1),
        ],
        out_specs=row_blk(0, 1),
        out_shape=jax.ShapeDtypeStruct((b, seq, W_BR), BF16),
        scratch_shapes=_two_stage_scratch(tq, seq, ATTN_TK),
        compiler_params=pltpu.CompilerParams(dimension_semantics=("arbitrary",),
                                             vmem_limit_bytes=VMEM_LIMIT),
        name="gqa_attn",
    )(u, u, vt, u)


def _na_block_type(bi, nblk):
    return jnp.where(bi < 2, bi, jnp.where(bi <= nblk - 3, 2, bi - (nblk - 5)))


def _na_kernel(q_ref, k_ref, vt_ref, g_ref, tab_ref, o_ref, *, rows):
    big = pl.program_id(1)
    nsub = q_ref.shape[0] // NA_QBLK
    nblk = rows // 2
    nkeys = NA_BAND * GRID_W
    left = lax.broadcasted_iota(jnp.int32, (NA_QBLK, LANES), 1) < DH

    def slices(sub, pair):
        bi = big * nsub + sub
        b0 = jnp.clip(2 * bi - NA_ROWS // 2, 0, rows - NA_BAND)
        ks = pl.ds(pl.multiple_of(b0 * GRID_W, 2 * GRID_W), nkeys)
        return (_na_block_type(bi, nblk), ks, slice(sub * NA_QBLK, (sub + 1) * NA_QBLK),
                slice(pair * LANES, (pair + 1) * LANES))

    def scores(sub, pair):
        typ, ks, rs, cs = slices(sub, pair)
        qq = jnp.concatenate(_lane_halves(q_ref[rs, cs]), axis=0)
        return _dot_nt(k_ref[ks, cs], qq) + tab_ref[typ, pair]

    def weights(s):
        return jnp.exp2(s - jnp.max(s, axis=0, keepdims=True)).astype(BF16)

    def finish(sub, pair, p):
        _, ks, rs, cs = slices(sub, pair)
        ot = _dot(_with_ones(vt_ref[pair, :, ks]), p)
        o = (ot[:LANES] / ot[LANES:LANES + 1]).T
        o = jnp.where(left, o[:NA_QBLK], o[NA_QBLK:])
        o_ref[rs, cs] = (o * _silu(g_ref[rs, cs].astype(F32))).astype(BF16)

    units = [(sub, pair) for sub in range(nsub) for pair in range(H_C // 2)]
    s_prev, p_prev = None, None
    for n in range(len(units) + 2):
        s_new = scores(*units[n]) if n < len(units) else None
        p_new = weights(s_prev) if s_prev is not None else None
        if p_prev is not None:
            finish(*units[n - 2], p_prev)
        s_prev, p_prev = s_new, p_new


def _na_attn(u, vt, na_tab, layer, tq):
    b, seq, _ = u.shape
    rows = seq // GRID_W
    assert (rows - NA_BAND) % 2 == 0
    row_blk = lambda off: pl.BlockSpec((None, tq, W_BR), lambda bi, i: (bi, i, off // W_BR))
    seq_blk = lambda off: pl.BlockSpec((None, seq, W_BR), lambda bi, i: (bi, 0, off // W_BR))
    return pl.pallas_call(
        functools.partial(_na_kernel, rows=rows),
        grid=(b, seq // tq),
        in_specs=[
            row_blk(OFF_CQ), seq_blk(OFF_CK),
            pl.BlockSpec((None,) + vt.shape[1:], lambda bi, i: (bi, 0, 0, 0)),
            row_blk(OFF_CG),
            pl.BlockSpec((None,) + na_tab.shape[1:], lambda bi, i: (layer, 0, 0, 0, 0),
                         pipeline_mode=pl.Buffered(1)),
        ],
        out_specs=pl.BlockSpec((None, tq, W_BR), lambda bi, i: (bi, i, 0)),
        out_shape=jax.ShapeDtypeStruct((b, seq, W_BR), BF16),
        compiler_params=pltpu.CompilerParams(dimension_semantics=("arbitrary", "arbitrary"),
                                             vmem_limit_bytes=VMEM_LIMIT),
        name="nbr_attn",
    )(u, u, vt, u, na_tab)


def _merge_kernel(x_ref, ya_ref, yb_ref, yc_ref, mg_ref, p_ref, wa_ref, wb_ref, wc_ref, wo_ref,
                  wpg_ref, wpe_ref, pg_ref, o_ref):
    def gate(k):
        return jax.nn.sigmoid(mg_ref[:, k * D_MODEL:(k + 1) * D_MODEL].astype(F32))

    m = gate(0) * _dot(ya_ref[...], wa_ref[...])
    m = m + gate(1) * _dot(yb_ref[...], wb_ref[...])
    m = m + gate(2) * _dot(yc_ref[...], wc_ref[...])
    r = _dot(m.astype(BF16), wo_ref[...])
    r = r * lax.rsqrt(jnp.mean(r * r, axis=-1, keepdims=True) + EPS) * pg_ref[...]
    x = x_ref[...] + r
    emb = _dot(p_ref[...].astype(BF16), wpe_ref[...])
    o_ref[...] = x + jax.nn.sigmoid(_dot(x.astype(BF16), wpg_ref[...])) * emb


def _merge(x, ya, yb, yc, u, p, wa, wb, wc, wo, wpg, wpe, post_g, layer, tm):
    n = x.shape[0]
    tok = lambda width: pl.BlockSpec((tm, width), lambda t: (t, 0))
    wgt = lambda r, c: pl.BlockSpec((None, r, c), lambda t: (layer, 0, 0))
    return pl.pallas_call(
        _merge_kernel,
        grid=(n // tm,),
        in_specs=[
            tok(D_MODEL), tok(W_BR), tok(W_BR), tok(W_BR),
            pl.BlockSpec((tm, 3 * D_MODEL), lambda t: (t, OFF_MG // (3 * D_MODEL))),
            pl.BlockSpec((None, tm, D_PLE), lambda t: (layer, t, 0)),
            wgt(W_BR, D_MODEL), wgt(W_BR, D_MODEL), wgt(W_BR, D_MODEL),
            wgt(D_MODEL, D_MODEL), wgt(D_MODEL, D_MODEL), wgt(D_PLE, D_MODEL),
            wgt(1, D_MODEL),
        ],
        out_specs=tok(D_MODEL),
        out_shape=jax.ShapeDtypeStruct((n, D_MODEL), F32),
        compiler_params=pltpu.CompilerParams(dimension_semantics=("arbitrary",),
                                             vmem_limit_bytes=VMEM_LIMIT),
        name="merge",
    )(x, ya, yb, yc, u, p, wa, wb, wc, wo, wpg, wpe, post_g)


def _t5_bucket(rel):
    nb = T5_BUCKETS // 2
    ret = jnp.where(rel > 0, nb, 0)
    n = jnp.abs(rel)
    max_exact = nb // 2
    nf = jnp.maximum(n, 1).astype(F32)
    large = max_exact + (jnp.log(nf / max_exact) / math.log(T5_MAX_DIST / max_exact)
                         * (nb - max_exact)).astype(jnp.int32)
    large = jnp.minimum(large, nb - 1)
    return ret + jnp.where(n < max_exact, n, large)


def _t5_tiles(t5_table, seq, tq, tk):
    assert tk >= T5_MAX_DIST and tq % tk == 0
    rel = jnp.arange(-(seq - 1), seq)
    rel_bias = t5_table[_t5_bucket(rel)].astype(F32).T * LOG2E
    big = tq + (tq // tk + 2) * tk
    padded = jnp.pad(rel_bias, ((0, 0), (big, big)), mode="edge")
    period = tq + tk
    tiles = []
    for e in range(-2, tq // tk + 2):
        base = seq - 1 + big + e * tk
        seg = padded[:, base - tq:base + tk]
        w = jnp.concatenate([seg[:, tq:], seg[:, :tq]], axis=1)
        toe = jnp.tile(w, (1, tq))[:, :tq * (period - 1)].reshape(-1, tq, period - 1)
        tiles.append(toe[:, :, :tk])
    return jnp.swapaxes(jnp.stack(tiles), 2, 3)


def _rope_tables(seq):
    t = jnp.arange(seq)
    row = (t // GRID_W).astype(F32)
    col = (t % GRID_W).astype(F32)
    n_freq = DH // 4
    inv = ROPE_THETA ** (-jnp.arange(n_freq, dtype=F32) / n_freq)
    ang = jnp.concatenate([row[:, None] * inv] * 2 + [col[:, None] * inv] * 2, axis=1)
    cos, sin = jnp.cos(ang), jnp.sin(ang)
    first_half = (jnp.arange(DH) % (2 * n_freq)) < n_freq
    sin_lo = jnp.where(first_half, -sin, 0.0)
    sin_hi = jnp.where(first_half, 0.0, sin)
    two = lambda a: jnp.concatenate([a, a], axis=1)
    return two(cos), two(sin_lo), two(sin_hi)


def _na_tables(rpb, seq):
    rows = seq // GRID_W
    nblk = rows // 2
    assert rows >= NA_BAND and nblk >= 5
    wr = min(NA_ROWS, rows)
    depth = rpb.shape[0]
    ncol = 2 * NA_COLS - 1
    bis = jnp.array([0, 1, 2, nblk - 2, nblk - 1])[:, None, None, None, None]
    qr = 2 * bis + jnp.arange(2)[None, :, None, None, None]
    qc = jnp.arange(GRID_W)[None, None, :, None, None]
    kr = (jnp.clip(2 * bis - NA_ROWS // 2, 0, rows - NA_BAND)
          + jnp.arange(NA_BAND)[None, None, None, :, None])
    kc = jnp.arange(GRID_W)[None, None, None, None, :]
    sr = jnp.clip(qr - wr // 2, 0, rows - wr)
    sc = jnp.clip(qc - NA_COLS // 2, 0, GRID_W - NA_COLS)
    mask = (kr >= sr) & (kr < sr + wr) & (kc >= sc) & (kc < sc + NA_COLS)
    ri = jnp.clip(kr - qr + NA_ROWS - 1, 0, 2 * NA_ROWS - 2)[:, :, 0, :, 0]
    ci = jnp.clip(kc - qc + NA_COLS - 1, 0, ncol - 1)[0, 0, :, 0, :]
    by_row = rpb.astype(F32)[:, :, ri, :]
    by_row = by_row.reshape(depth, H_C // 2, 2, 5, 2, NA_BAND, ncol)
    by_row = jnp.transpose(by_row, (0, 3, 1, 5, 2, 4, 6))
    ea = jnp.arange(4)
    same_ea = (ea[:, None, None, None, None] == ea[None, None, None, :, None])
    hit = (jnp.arange(ncol)[None, :, None, None, None] == ci.T[None, None, :, None, :])
    onehot = (same_ea & hit).astype(F32).reshape(4 * ncol, GRID_W * 4 * GRID_W)
    bias = jnp.dot(by_row.reshape(-1, 4 * ncol), onehot, precision=lax.Precision.HIGHEST)
    bias = bias.reshape(depth, 5, H_C // 2, NA_BAND, GRID_W, 2, 2, GRID_W)
    mask = jnp.transpose(mask, (0, 3, 4, 1, 2))
    bias = jnp.where(mask[None, :, None, :, :, None], bias * LOG2E, -jnp.inf)
    return bias.reshape(depth, 5, H_C // 2, NA_BAND * GRID_W, 2 * NA_QBLK)


def _prepare(t5_table, pre_norm_g, w_in, subln_g, q_norm_g, k_norm_g, na_rpb, w_branch_a, w_branch_b,
             w_branch_c, w_out, post_norm_g, w_ple_proj, w_ple_gate, seq, tq):
    depth = w_in.shape[0]
    two = lambda a: jnp.concatenate([a, a], axis=-1).reshape(depth, 1, LANES)
    return dict(
        w_in=_permute_in_proj(w_in).astype(BF16),
        pre_g=pre_norm_g.reshape(depth, 1, D_MODEL),
        qn=two(q_norm_g), kn=two(k_norm_g),
        subln=subln_g.reshape(depth, 1, LANES),
        wa=w_branch_a.astype(BF16), wb=_pair_major_heads(w_branch_b, 1).astype(BF16), wc=w_branch_c.astype(BF16),
        wo=w_out.astype(BF16), wpg=w_ple_gate.astype(BF16), wpe=w_ple_proj.astype(BF16),
        post_g=post_norm_g.reshape(depth, 1, D_MODEL),
        t5=_t5_tiles(t5_table, seq, tq, T5_TK), rope=_rope_tables(seq), na=_na_tables(na_rpb, seq),
    )


def _trunk(x, p, lambda_qk, w, *, tm, tq_a, tq_b, tq_c):
    b, seq, _ = x.shape
    depth = p.shape[0]
    n = b * seq
    xf = x.reshape(n, D_MODEL)
    pf = p.reshape(depth, n, D_PLE)
    for layer in range(depth):
        u, vta, vtb, vtc = _in_proj(xf, w["pre_g"], w["w_in"], w["qn"], w["kn"], w["rope"], layer, seq, tm)
        u3 = u.reshape(b, seq, U_W)
        ya = _diff_attn(u3, vta, lambda_qk, w["subln"], w["t5"], layer, tq_a)
        yb = _gqa_attn(u3, vtb, tq_b)
        yc = _na_attn(u3, vtc, w["na"], layer, tq_c)
        xf = _merge(xf, ya.reshape(n, W_BR), yb.reshape(n, W_BR), yc.reshape(n, W_BR), u, pf,
                    w["wa"], w["wb"], w["wc"], w["wo"], w["wpg"], w["wpe"], w["post_g"], layer, tm)
    return xf.reshape(b, seq, D_MODEL)


TILES = dict(tm=512, tq_a=512, tq_b=512, tq_c=512)


def kernel(x_prompt, x_sample, p_prompt, p_sample, t5_table, pre_norm_g, w_in, lambda_qk, subln_g, q_norm_g,
           k_norm_g, na_rpb, w_branch_a, w_branch_b, w_branch_c, w_out, post_norm_g, w_ple_proj, w_ple_gate):
    assert x_prompt.shape[1] == x_sample.shape[1]
    w = _prepare(t5_table, pre_norm_g, w_in, subln_g, q_norm_g, k_norm_g, na_rpb, w_branch_a, w_branch_b,
                 w_branch_c, w_out, post_norm_g, w_ple_proj, w_ple_gate, x_prompt.shape[1], TILES["tq_a"])
    y_prompt = _trunk(x_prompt, p_prompt, lambda_qk, w, **TILES)
    y_sample = _trunk(x_sample, p_sample, lambda_qk, w, **TILES)
    return (y_prompt, y_sample)
```

```python
import functools
import math

import jax
import jax.numpy as jnp
from jax import lax
from jax.experimental import pallas as pl
from jax.experimental.pallas import tpu as pltpu

F32 = jnp.float32
BF16 = jnp.bfloat16

D_MODEL = 1024
D_PLE = 256
EPS = 1e-6
GRID_W = 64
LANES = 128
SUBLANES = 8
ONES_ROWS = 16
DH = 64
H_A = 4
H_B = 8
KV_B = 2
REP_B = H_B // KV_B
H_C = 8
W_BR = 512
NA_ROWS = 8
NA_COLS = 16
NA_BAND = 10
NA_QBLK = 2 * GRID_W
T5_BUCKETS = 32
T5_MAX_DIST = 128
ATTN_TK = 256
T5_TK = ATTN_TK
ROPE_THETA = 10000.0
LOG2E = math.log2(math.e)
QK_SCALE = DH ** -0.5 * LOG2E

OFF_MG = 0
OFF_AQ, OFF_AK, OFF_AG = 3072, 3584, 4096
OFF_BQ, OFF_BG = 4608, 5120
OFF_CQ, OFF_CK, OFF_CG = 5632, 6144, 6656
OFF_BK = 7168
U_W = OFF_BK + LANES
W_AV, W_CV, W_BKV = 7168, 7680, 8192
IN_W = 8448

VMEM_LIMIT = 56 * 1024 * 1024


def _pair_major_heads(w, axis):
    shp = w.shape
    w = w.reshape(shp[:axis] + (KV_B, REP_B, DH) + shp[axis + 1:])
    return jnp.swapaxes(w, axis, axis + 1).reshape(shp)


def _permute_in_proj(w_in):
    sizes = dict(aq=512, ak=512, av=512, ag=512, bq=512, bk=128, bv=128, bg=512, cq=512, ck=512, cv=512,
                 cg=512, mg=3072)
    seg, off = {}, 0
    for name, size in sizes.items():
        seg[name] = w_in[:, :, off:off + size]
        off += size
    seg["aq"] = seg["aq"] * QK_SCALE
    seg["cq"] = seg["cq"] * QK_SCALE
    seg["bq"] = _pair_major_heads(seg["bq"], 2)
    seg["bg"] = _pair_major_heads(seg["bg"], 2)
    order = ("mg", "aq", "ak", "ag", "bq", "bg", "cq", "ck", "cg", "av", "cv", "bk", "bv")
    return jnp.concatenate([seg[n] for n in order], axis=2)


def _silu(x):
    return x * jax.nn.sigmoid(x)


def _dot_nt(a, b):
    return lax.dot_general(a, b, (((1,), (1,)), ((), ())), preferred_element_type=F32)


def _dot(a, b):
    return jnp.dot(a, b, preferred_element_type=F32)


def _pair_norm_rope(a, gain, cos, sin_lo, sin_hi, left):
    sq = a * a
    s_l = jnp.sum(jnp.where(left, sq, 0.0), axis=-1, keepdims=True)
    s_r = jnp.sum(jnp.where(left, 0.0, sq), axis=-1, keepdims=True)
    ms = jnp.where(left, s_l, s_r) * (1.0 / DH)
    y = a * lax.rsqrt(ms + EPS) * gain
    return y * cos + pltpu.roll(y, 112, 1) * sin_lo + pltpu.roll(y, 16, 1) * sin_hi


def _in_proj_kernel(x_ref, g_ref, w_ref, qn_ref, kn_ref, cos_ref, slo_ref, shi_ref,
                    u_ref, vta_ref, vtb_ref, vtc_ref):
    x = x_ref[...]
    ms = jnp.mean(x * x, axis=-1, keepdims=True)
    h = (x * lax.rsqrt(ms + EPS) * g_ref[...]).astype(BF16)
    tm = x.shape[0]
    left = lax.broadcasted_iota(jnp.int32, (tm, LANES), 1) < DH
    cos, slo, shi = cos_ref[...], slo_ref[...], shi_ref[...]
    acc = _dot(h, w_ref[:, W_BKV:IN_W])
    bk = _pair_norm_rope(acc[:, :LANES], kn_ref[...], cos, slo, shi, left)
    u_ref[:, OFF_BK:U_W] = bk.astype(BF16)
    vtb_ref[0] = acc[:, LANES:].T.astype(BF16)
    acc = _dot(h, w_ref[:, OFF_BQ:OFF_BQ + W_BR])
    for b in range(W_BR // LANES):
        blk = _pair_norm_rope(acc[:, b * LANES:(b + 1) * LANES], qn_ref[...], cos, slo, shi, left)
        u_ref[:, OFF_BQ + b * LANES:OFF_BQ + (b + 1) * LANES] = (blk * QK_SCALE).astype(BF16)
    for c0, vt_ref in ((W_AV, vta_ref), (W_CV, vtc_ref)):
        vt = _dot(h, w_ref[:, c0:c0 + W_BR]).T
        vt_ref[...] = vt.reshape(W_BR // LANES, LANES, tm).astype(BF16)
    for c0 in range(0, OFF_BK, W_BR):
        if c0 != OFF_BQ:
            u_ref[:, c0:c0 + W_BR] = _dot(h, w_ref[:, c0:c0 + W_BR]).astype(BF16)


def _in_proj(x, pre_g, w_in, qn, kn, rope, layer, seq, tm):
    n = x.shape[0]
    nseq = seq // tm
    cos, slo, shi = rope
    vec = lambda width: pl.BlockSpec((None, 1, width), lambda t: (layer, 0, 0))
    tab = pl.BlockSpec((tm, LANES), lambda t: (t % nseq, 0))
    vt_spec = lambda heads: pl.BlockSpec((None, heads, LANES, tm), lambda t: (t // nseq, 0, 0, t % nseq))
    vt_shape = lambda heads: jax.ShapeDtypeStruct((n // seq, heads, LANES, seq), BF16)
    return pl.pallas_call(
        _in_proj_kernel,
        grid=(n // tm,),
        in_specs=[
            pl.BlockSpec((tm, D_MODEL), lambda t: (t, 0)),
            vec(D_MODEL),
            pl.BlockSpec((None, D_MODEL, IN_W), lambda t: (layer, 0, 0), pipeline_mode=pl.Buffered(1)),
            vec(LANES), vec(LANES), tab, tab, tab,
        ],
        out_specs=[pl.BlockSpec((tm, U_W), lambda t: (t, 0)), vt_spec(H_A), vt_spec(1), vt_spec(H_C // 2)],
        out_shape=[jax.ShapeDtypeStruct((n, U_W), BF16), vt_shape(H_A), vt_shape(1), vt_shape(H_C // 2)],
        compiler_params=pltpu.CompilerParams(dimension_semantics=("arbitrary",),
                                             vmem_limit_bytes=VMEM_LIMIT),
        name="in_proj",
    )(x, pre_g, w_in, qn, kn, cos, slo, shi)


def _two_stage_scratch(tq, seq, tk):
    return [pltpu.VMEM((2, seq // tk, tk, tq), F32), pltpu.VMEM((2, SUBLANES, tq), F32)] * 2


def _two_stage_step(qs, k_ref, vt_ref, bias, s_w, m_w, s_r, m_r):
    _, nchunk, tk, tq = s_w.shape
    slabs = tk // SUBLANES
    m_run, acc = [None] * 2, [None] * 2
    for c in range(nchunk):
        kc = k_ref[c * tk:(c + 1) * tk, :]
        vtc = _with_ones(vt_ref[:, c * tk:(c + 1) * tk])
        for e in range(2):
            p = jnp.exp2(s_r[e, c].reshape(slabs, SUBLANES, tq) - m_r[e][None])
            pv = _dot(vtc, p.reshape(tk, tq).astype(BF16))
            acc[e] = pv if c == 0 else acc[e] + pv
            s = _dot_nt(kc, qs[e])
            if bias is not None:
                s = s + bias(c, e)
            s_w[e, c] = s
            m_c = jnp.max(s.reshape(slabs, SUBLANES, tq), axis=0)
            m_run[e] = m_c if c == 0 else jnp.maximum(m_run[e], m_c)
    outs = []
    for e in range(2):
        m_w[e] = jnp.broadcast_to(jnp.max(m_run[e], axis=0, keepdims=True), (SUBLANES, tq))
        outs.append((acc[e][:LANES] / acc[e][LANES:LANES + 1]).T)
    return outs


def _two_stage(step, scratch):
    s_a, m_a, s_b, m_b = scratch
    t = pl.program_id(0)

    @pl.when(t == 0)
    def _():
        s_b[...] = jnp.zeros_like(s_b)
        m_b[...] = jnp.zeros_like(m_b)

    @pl.when(t % 2 == 0)
    def _():
        step(s_a, m_a, s_b, m_b)

    @pl.when(t % 2 == 1)
    def _():
        step(s_b, m_b, s_a, m_a)


def _lane_halves(q):
    left = lax.broadcasted_iota(jnp.int32, q.shape, 1) < DH
    zero = jnp.zeros_like(q)
    return jnp.where(left, q, zero), jnp.where(left, zero, q)


def _unit_maps(b, heads, n_i):
    units = b * heads * n_i

    def unit(t, lag):
        un = jnp.clip(t - lag, 0, units - 1)
        return un // (heads * n_i), (un // n_i) % heads, un % n_i

    return units, unit


def _diff_attn_kernel(lam_ref, sg_ref, q_ref, k_ref, vt_ref, g_ref, t_ref, o_ref, *scratch,
                      lambda_init, n_i, units):
    i = jnp.minimum(pl.program_id(0), units - 1) % n_i
    per_q = q_ref.shape[0] // t_ref.shape[2]

    def bias(c, mp):
        return t_ref[jnp.clip(c - i * per_q, -2, per_q + 1) + 2, mp]

    def step(*bufs):
        lp = lam_ref[...]
        lam = (jnp.exp(jnp.sum(lp[0:1] * lp[1:2], keepdims=True))
               - jnp.exp(jnp.sum(lp[2:3] * lp[3:4], keepdims=True)) + lambda_init)
        o0, o1 = _two_stage_step(_lane_halves(q_ref[...]), k_ref, vt_ref, bias, *bufs)
        o = o0 - lam * o1
        o = o * lax.rsqrt(jnp.mean(o * o, axis=-1, keepdims=True) + EPS) * sg_ref[...]
        o = o * (1.0 - lambda_init)
        o_ref[...] = (o * _silu(g_ref[...].astype(F32))).astype(BF16)

    _two_stage(step, scratch)


def _with_ones(vt):
    return jnp.concatenate([vt, jnp.ones((ONES_ROWS, vt.shape[1]), vt.dtype)], axis=0)


def _diff_attn(u, vt, lam_qk, subln_g, t5_tiles, layer, tq):
    b, seq, _ = u.shape
    ntile, _, tk, _ = t5_tiles.shape
    n_i = seq // tq
    units, unit = _unit_maps(b, H_A, n_i)
    lambda_init = 0.8 - 0.6 * math.exp(-0.3 * layer)

    def row_blk(off, lag):
        def index(t):
            bi, h, i = unit(t, lag)
            return bi, i, off // LANES + h
        return pl.BlockSpec((None, tq, LANES), index)

    def seq_blk(off, lag):
        def index(t):
            bi, h, _ = unit(t, lag)
            return bi, 0, off // LANES + h
        return pl.BlockSpec((None, seq, LANES), index)

    return pl.pallas_call(
        functools.partial(_diff_attn_kernel, lambda_init=lambda_init, n_i=n_i, units=units),
        grid=(units + 1,),
        in_specs=[
            pl.BlockSpec((None, 4, DH), lambda t: (layer, 0, 0)),
            pl.BlockSpec((None, 1, LANES), lambda t: (layer, 0, 0)),
            row_blk(OFF_AQ, 0), seq_blk(OFF_AK, 0),
            pl.BlockSpec((None, None, LANES, seq), lambda t: unit(t, 1)[:2] + (0, 0)),
            row_blk(OFF_AG, 1),
            pl.BlockSpec((ntile, 2, tk, tq), lambda t: (0, unit(t, 0)[1], 0, 0)),
        ],
        out_specs=row_blk(0, 1),
        out_shape=jax.ShapeDtypeStruct((b, seq, W_BR), BF16),
        scratch_shapes=_two_stage_scratch(tq, seq, tk),
        compiler_params=pltpu.CompilerParams(dimension_semantics=("arbitrary",),
                                             vmem_limit_bytes=VMEM_LIMIT),
        name="diff_attn",
    )(lam_qk, subln_g, u, u, vt, u, t5_tiles)


def _gqa_kernel(q_ref, k_ref, vt_ref, g_ref, o_ref, *scratch):
    def step(*bufs):
        o0, o1 = _two_stage_step(_lane_halves(q_ref[...]), k_ref, vt_ref, None, *bufs)
        left = lax.broadcasted_iota(jnp.int32, o0.shape, 1) < DH
        o = jnp.where(left, o0, o1)
        o_ref[...] = (o * _silu(g_ref[...].astype(F32))).astype(BF16)

    _two_stage(step, scratch)


def _gqa_attn(u, vt, tq):
    b, seq, _ = u.shape
    units, unit = _unit_maps(b, REP_B, seq // tq)

    def row_blk(off, lag):
        def index(t):
            bi, pr, i = unit(t, lag)
            return bi, i, off // LANES + pr
        return pl.BlockSpec((None, tq, LANES), index)

    def seq_blk(off, lag):
        return pl.BlockSpec((None, seq, LANES), lambda t: (unit(t, lag)[0], 0, off // LANES))

    return pl.pallas_call(
        _gqa_kernel,
        grid=(units + 1,),
        in_specs=[
            row_blk(OFF_BQ, 0), seq_blk(OFF_BK, 0),
            pl.BlockSpec((None, None, LANES, seq), lambda t: (unit(t, 1)[0], 0, 0, 0)),
            row_blk(OFF_BG, 1),
        ],
        out_specs=row_blk(0, 1),
        out_shape=jax.ShapeDtypeStruct((b, seq, W_BR), BF16),
        scratch_shapes=_two_stage_scratch(tq, seq, ATTN_TK),
        compiler_params=pltpu.CompilerParams(dimension_semantics=("arbitrary",),
                                             vmem_limit_bytes=VMEM_LIMIT),
        name="gqa_attn",
    )(u, u, vt, u)


def _na_block_type(bi, nblk):
    return jnp.where(bi < 2, bi, jnp.where(bi <= nblk - 3, 2, bi - (nblk - 5)))


def _na_kernel(q_ref, k_ref, vt_ref, g_ref, tab_ref, o_ref, *, rows):
    big = pl.program_id(1)
    nsub = q_ref.shape[0] // NA_QBLK
    nblk = rows // 2
    nkeys = NA_BAND * GRID_W
    left = lax.broadcasted_iota(jnp.int32, (NA_QBLK, LANES), 1) < DH

    def slices(sub, pair):
        bi = big * nsub + sub
        b0 = jnp.clip(2 * bi - NA_ROWS // 2, 0, rows - NA_BAND)
        ks = pl.ds(pl.multiple_of(b0 * GRID_W, 2 * GRID_W), nkeys)
        return (_na_block_type(bi, nblk), ks, slice(sub * NA_QBLK, (sub + 1) * NA_QBLK),
                slice(pair * LANES, (pair + 1) * LANES))

    def scores(sub, pair):
        typ, ks, rs, cs = slices(sub, pair)
        qq = jnp.concatenate(_lane_halves(q_ref[rs, cs]), axis=0)
        return _dot_nt(k_ref[ks, cs], qq) + tab_ref[typ, pair]

    def weights(s):
        return jnp.exp2(s - jnp.max(s, axis=0, keepdims=True)).astype(BF16)

    def finish(sub, pair, p):
        _, ks, rs, cs = slices(sub, pair)
        ot = _dot(_with_ones(vt_ref[pair, :, ks]), p)
        o = (ot[:LANES] / ot[LANES:LANES + 1]).T
        o = jnp.where(left, o[:NA_QBLK], o[NA_QBLK:])
        o_ref[rs, cs] = (o * _silu(g_ref[rs, cs].astype(F32))).astype(BF16)

    units = [(sub, pair) for sub in range(nsub) for pair in range(H_C // 2)]
    s_prev, p_prev = None, None
    for n in range(len(units) + 2):
        s_new = scores(*units[n]) if n < len(units) else None
        p_new = weights(s_prev) if s_prev is not None else None
        if p_prev is not None:
            finish(*units[n - 2], p_prev)
        s_prev, p_prev = s_new, p_new


def _na_attn(u, vt, na_tab, layer, tq):
    b, seq, _ = u.shape
    rows = seq // GRID_W
    assert (rows - NA_BAND) % 2 == 0
    row_blk = lambda off: pl.BlockSpec((None, tq, W_BR), lambda bi, i: (bi, i, off // W_BR))
    seq_blk = lambda off: pl.BlockSpec((None, seq, W_BR), lambda bi, i: (bi, 0, off // W_BR))
    return pl.pallas_call(
        functools.partial(_na_kernel, rows=rows),
        grid=(b, seq // tq),
        in_specs=[
            row_blk(OFF_CQ), seq_blk(OFF_CK),
            pl.BlockSpec((None,) + vt.shape[1:], lambda bi, i: (bi, 0, 0, 0)),
            row_blk(OFF_CG),
            pl.BlockSpec((None,) + na_tab.shape[1:], lambda bi, i: (layer, 0, 0, 0, 0),
                         pipeline_mode=pl.Buffered(1)),
        ],
        out_specs=pl.BlockSpec((None, tq, W_BR), lambda bi, i: (bi, i, 0)),
        out_shape=jax.ShapeDtypeStruct((b, seq, W_BR), BF16),
        compiler_params=pltpu.CompilerParams(dimension_semantics=("arbitrary", "arbitrary"),
                                             vmem_limit_bytes=VMEM_LIMIT),
        name="nbr_attn",
    )(u, u, vt, u, na_tab)


def _merge_kernel(x_ref, ya_ref, yb_ref, yc_ref, mg_ref, p_ref, wa_ref, wb_ref, wc_ref, wo_ref,
                  wpg_ref, wpe_ref, pg_ref, o_ref):
    def gate(k):
        return jax.nn.sigmoid(mg_ref[:, k * D_MODEL:(k + 1) * D_MODEL].astype(F32))

    m = gate(0) * _dot(ya_ref[...], wa_ref[...])
    m = m + gate(1) * _dot(yb_ref[...], wb_ref[...])
    m = m + gate(2) * _dot(yc_ref[...], wc_ref[...])
    r = _dot(m.astype(BF16), wo_ref[...])
    r = r * lax.rsqrt(jnp.mean(r * r, axis=-1, keepdims=True) + EPS) * pg_ref[...]
    x = x_ref[...] + r
    emb = _dot(p_ref[...].astype(BF16), wpe_ref[...])
    o_ref[...] = x + jax.nn.sigmoid(_dot(x.astype(BF16), wpg_ref[...])) * emb


def _merge(x, ya, yb, yc, u, p, wa, wb, wc, wo, wpg, wpe, post_g, layer, tm):
    n = x.shape[0]
    tok = lambda width: pl.BlockSpec((tm, width), lambda t: (t, 0))
    wgt = lambda r, c: pl.BlockSpec((None, r, c), lambda t: (layer, 0, 0))
    return pl.pallas_call(
        _merge_kernel,
        grid=(n // tm,),
        in_specs=[
            tok(D_MODEL), tok(W_BR), tok(W_BR), tok(W_BR),
            pl.BlockSpec((tm, 3 * D_MODEL), lambda t: (t, OFF_MG // (3 * D_MODEL))),
            pl.BlockSpec((None, tm, D_PLE), lambda t: (layer, t, 0)),
            wgt(W_BR, D_MODEL), wgt(W_BR, D_MODEL), wgt(W_BR, D_MODEL),
            wgt(D_MODEL, D_MODEL), wgt(D_MODEL, D_MODEL), wgt(D_PLE, D_MODEL),
            wgt(1, D_MODEL),
        ],
        out_specs=tok(D_MODEL),
        out_shape=jax.ShapeDtypeStruct((n, D_MODEL), F32),
        compiler_params=pltpu.CompilerParams(dimension_semantics=("arbitrary",),
                                             vmem_limit_bytes=VMEM_LIMIT),
        name="merge",
    )(x, ya, yb, yc, u, p, wa, wb, wc, wo, wpg, wpe, post_g)


def _t5_bucket(rel):
    nb = T5_BUCKETS // 2
    ret = jnp.where(rel > 0, nb, 0)
    n = jnp.abs(rel)
    max_exact = nb // 2
    nf = jnp.maximum(n, 1).astype(F32)
    large = max_exact + (jnp.log(nf / max_exact) / math.log(T5_MAX_DIST / max_exact)
                         * (nb - max_exact)).astype(jnp.int32)
    large = jnp.minimum(large, nb - 1)
    return ret + jnp.where(n < max_exact, n, large)


def _t5_tiles(t5_table, seq, tq, tk):
    assert tk >= T5_MAX_DIST and tq % tk == 0
    rel = jnp.arange(-(seq - 1), seq)
    rel_bias = t5_table[_t5_bucket(rel)].astype(F32).T * LOG2E
    big = tq + (tq // tk + 2) * tk
    padded = jnp.pad(rel_bias, ((0, 0), (big, big)), mode="edge")
    period = tq + tk
    tiles = []
    for e in range(-2, tq // tk + 2):
        base = seq - 1 + big + e * tk
        rev = padded[:, base - tq:base + tk][:, ::-1]
        w = jnp.concatenate([rev[:, tk - 1:], rev[:, :tk - 1]], axis=1)
        toe = jnp.tile(w, (1, tk))[:, :tk * (period - 1)].reshape(-1, tk, period - 1)
        tiles.append(toe[:, :, :tq])
    return jnp.stack(tiles)


def _rope_tables(seq):
    t = jnp.arange(seq)
    row = (t // GRID_W).astype(F32)
    col = (t % GRID_W).astype(F32)
    n_freq = DH // 4
    inv = ROPE_THETA ** (-jnp.arange(n_freq, dtype=F32) / n_freq)
    ang = jnp.concatenate([row[:, None] * inv] * 2 + [col[:, None] * inv] * 2, axis=1)
    cos, sin = jnp.cos(ang), jnp.sin(ang)
    first_half = (jnp.arange(DH) % (2 * n_freq)) < n_freq
    sin_lo = jnp.where(first_half, -sin, 0.0)
    sin_hi = jnp.where(first_half, 0.0, sin)
    two = lambda a: jnp.concatenate([a, a], axis=1)
    return two(cos), two(sin_lo), two(sin_hi)


def _na_tables(rpb, seq):
    rows = seq // GRID_W
    nblk = rows // 2
    assert rows >= NA_BAND and nblk >= 5
    wr = min(NA_ROWS, rows)
    depth = rpb.shape[0]
    ncol = 2 * NA_COLS - 1
    bis = jnp.array([0, 1, 2, nblk - 2, nblk - 1])[:, None, None, None, None]
    qr = 2 * bis + jnp.arange(2)[None, :, None, None, None]
    qc = jnp.arange(GRID_W)[None, None, :, None, None]
    kr = (jnp.clip(2 * bis - NA_ROWS // 2, 0, rows - NA_BAND)
          + jnp.arange(NA_BAND)[None, None, None, :, None])
    kc = jnp.arange(GRID_W)[None, None, None, None, :]
    sr = jnp.clip(qr - wr // 2, 0, rows - wr)
    sc = jnp.clip(qc - NA_COLS // 2, 0, GRID_W - NA_COLS)
    mask = (kr >= sr) & (kr < sr + wr) & (kc >= sc) & (kc < sc + NA_COLS)
    ri = jnp.clip(kr - qr + NA_ROWS - 1, 0, 2 * NA_ROWS - 2)[:, :, 0, :, 0]
    ci = jnp.clip(kc - qc + NA_COLS - 1, 0, ncol - 1)[0, 0, :, 0, :]
    by_row = rpb.astype(F32)[:, :, ri, :]
    by_row = by_row.reshape(depth, H_C // 2, 2, 5, 2, NA_BAND, ncol)
    by_row = jnp.transpose(by_row, (0, 3, 1, 5, 2, 4, 6))
    ea = jnp.arange(4)
    same_ea = (ea[:, None, None, None, None] == ea[None, None, None, :, None])
    hit = (jnp.arange(ncol)[None, :, None, None, None] == ci.T[None, None, :, None, :])
    onehot = (same_ea & hit).astype(F32).reshape(4 * ncol, GRID_W * 4 * GRID_W)
    bias = jnp.dot(by_row.reshape(-1, 4 * ncol), onehot, precision=lax.Precision.HIGHEST)
    bias = bias.reshape(depth, 5, H_C // 2, NA_BAND, GRID_W, 2, 2, GRID_W)
    mask = jnp.transpose(mask, (0, 3, 4, 1, 2))
    bias = jnp.where(mask[None, :, None, :, :, None], bias * LOG2E, -jnp.inf)
    return bias.reshape(depth, 5, H_C // 2, NA_BAND * GRID_W, 2 * NA_QBLK)


def _prepare(t5_table, pre_norm_g, w_in, subln_g, q_norm_g, k_norm_g, na_rpb, w_branch_a, w_branch_b,
             w_branch_c, w_out, post_norm_g, w_ple_proj, w_ple_gate, seq, tq):
    depth = w_in.shape[0]
    two = lambda a: jnp.concatenate([a, a], axis=-1).reshape(depth, 1, LANES)
    return dict(
        w_in=_permute_in_proj(w_in).astype(BF16),
        pre_g=pre_norm_g.reshape(depth, 1, D_MODEL),
        qn=two(q_norm_g), kn=two(k_norm_g),
        subln=subln_g.reshape(depth, 1, LANES),
        wa=w_branch_a.astype(BF16), wb=_pair_major_heads(w_branch_b, 1).astype(BF16), wc=w_branch_c.astype(BF16),
        wo=w_out.astype(BF16), wpg=w_ple_gate.astype(BF16), wpe=w_ple_proj.astype(BF16),
        post_g=post_norm_g.reshape(depth, 1, D_MODEL),
        t5=_t5_tiles(t5_table, seq, tq, T5_TK), rope=_rope_tables(seq), na=_na_tables(na_rpb, seq),
    )


def _trunk(x, p, lambda_qk, w, *, tm, tq_a, tq_b, tq_c):
    b, seq, _ = x.shape
    depth = p.shape[0]
    n = b * seq
    xf = x.reshape(n, D_MODEL)
    pf = p.reshape(depth, n, D_PLE)
    for layer in range(depth):
        u, vta, vtb, vtc = _in_proj(xf, w["pre_g"], w["w_in"], w["qn"], w["kn"], w["rope"], layer, seq, tm)
        u3 = u.reshape(b, seq, U_W)
        ya = _diff_attn(u3, vta, lambda_qk, w["subln"], w["t5"], layer, tq_a)
        yb = _gqa_attn(u3, vtb, tq_b)
        yc = _na_attn(u3, vtc, w["na"], layer, tq_c)
        xf = _merge(xf, ya.reshape(n, W_BR), yb.reshape(n, W_BR), yc.reshape(n, W_BR), u, pf,
                    w["wa"], w["wb"], w["wc"], w["wo"], w["wpg"], w["wpe"], w["post_g"], layer, tm)
    return xf.reshape(b, seq, D_MODEL)


TILES = dict(tm=512, tq_a=512, tq_b=512, tq_c=1024)


def kernel(x_prompt, x_sample, p_prompt, p_sample, t5_table, pre_norm_g, w_in, lambda_qk, subln_g, q_norm_g,
           k_norm_g, na_rpb, w_branch_a, w_branch_b, w_branch_c, w_out, post_norm_g, w_ple_proj, w_ple_gate):
    assert x_prompt.shape[1] == x_sample.shape[1]
    w = _prepare(t5_table, pre_norm_g, w_in, subln_g, q_norm_g, k_norm_g, na_rpb, w_branch_a, w_branch_b,
                 w_branch_c, w_out, post_norm_g, w_ple_proj, w_ple_gate, x_prompt.shape[1], TILES["tq_a"])
    y_prompt = _trunk(x_prompt, p_prompt, lambda_qk, w, **TILES)
    y_sample = _trunk(x_sample, p_sample, lambda_qk, w, **TILES)
    return (y_prompt, y_sample)
```

```python
import functools
import math

import jax
import jax.numpy as jnp
from jax import lax
from jax.experimental import pallas as pl
from jax.experimental.pallas import tpu as pltpu

F32 = jnp.float32
BF16 = jnp.bfloat16

D_MODEL = 1024
D_PLE = 256
EPS = 1e-6
GRID_W = 64
LANES = 128
SUBLANES = 8
ONES_ROWS = 16
DH = 64
H_A = 4
H_B = 8
KV_B = 2
REP_B = H_B // KV_B
H_C = 8
W_BR = 512
NA_ROWS = 8
NA_COLS = 16
NA_BAND = 10
NA_QBLK = 2 * GRID_W
T5_BUCKETS = 32
T5_MAX_DIST = 128
ATTN_TK = 256
T5_TK = ATTN_TK
ROPE_THETA = 10000.0
LOG2E = math.log2(math.e)
QK_SCALE = DH ** -0.5 * LOG2E

OFF_MG = 0
OFF_AQ, OFF_AK, OFF_AG = 3072, 3584, 4096
OFF_BQ, OFF_BG = 4608, 5120
OFF_CQ, OFF_CK, OFF_CG = 5632, 6144, 6656
OFF_BK = 7168
U_W = OFF_BK + LANES
W_AV, W_CV, W_BKV = 7168, 7680, 8192
IN_W = 8448

VMEM_LIMIT = 56 * 1024 * 1024


def _pair_major_heads(w, axis):
    shp = w.shape
    w = w.reshape(shp[:axis] + (KV_B, REP_B, DH) + shp[axis + 1:])
    return jnp.swapaxes(w, axis, axis + 1).reshape(shp)


def _permute_in_proj(w_in):
    sizes = dict(aq=512, ak=512, av=512, ag=512, bq=512, bk=128, bv=128, bg=512, cq=512, ck=512, cv=512,
                 cg=512, mg=3072)
    seg, off = {}, 0
    for name, size in sizes.items():
        seg[name] = w_in[:, :, off:off + size]
        off += size
    seg["aq"] = seg["aq"] * QK_SCALE
    seg["cq"] = seg["cq"] * QK_SCALE
    seg["bq"] = _pair_major_heads(seg["bq"], 2)
    seg["bg"] = _pair_major_heads(seg["bg"], 2)
    order = ("mg", "aq", "ak", "ag", "bq", "bg", "cq", "ck", "cg", "av", "cv", "bk", "bv")
    return jnp.concatenate([seg[n] for n in order], axis=2)


def _silu(x):
    return x * jax.nn.sigmoid(x)


def _dot_nt(a, b):
    return lax.dot_general(a, b, (((1,), (1,)), ((), ())), preferred_element_type=F32)


def _dot(a, b):
    return jnp.dot(a, b, preferred_element_type=F32)


def _pair_norm_rope(a, gain, cos, sin_lo, sin_hi, left):
    sq = a * a
    s_l = jnp.sum(jnp.where(left, sq, 0.0), axis=-1, keepdims=True)
    s_r = jnp.sum(jnp.where(left, 0.0, sq), axis=-1, keepdims=True)
    ms = jnp.where(left, s_l, s_r) * (1.0 / DH)
    y = a * lax.rsqrt(ms + EPS) * gain
    return y * cos + pltpu.roll(y, 112, 1) * sin_lo + pltpu.roll(y, 16, 1) * sin_hi


def _in_proj_kernel(x_ref, g_ref, w_ref, qn_ref, kn_ref, cos_ref, slo_ref, shi_ref,
                    u_ref, vta_ref, vtb_ref, vtc_ref):
    x = x_ref[...]
    ms = jnp.mean(x * x, axis=-1, keepdims=True)
    h = (x * lax.rsqrt(ms + EPS) * g_ref[...]).astype(BF16)
    tm = x.shape[0]
    left = lax.broadcasted_iota(jnp.int32, (tm, LANES), 1) < DH
    cos, slo, shi = cos_ref[...], slo_ref[...], shi_ref[...]
    acc = _dot(h, w_ref[:, W_BKV:IN_W])
    bk = _pair_norm_rope(acc[:, :LANES], kn_ref[...], cos, slo, shi, left)
    u_ref[:, OFF_BK:U_W] = bk.astype(BF16)
    vtb_ref[0] = acc[:, LANES:].T.astype(BF16)
    acc = _dot(h, w_ref[:, OFF_BQ:OFF_BQ + W_BR])
    for b in range(W_BR // LANES):
        blk = _pair_norm_rope(acc[:, b * LANES:(b + 1) * LANES], qn_ref[...], cos, slo, shi, left)
        u_ref[:, OFF_BQ + b * LANES:OFF_BQ + (b + 1) * LANES] = (blk * QK_SCALE).astype(BF16)
    for c0, vt_ref in ((W_AV, vta_ref), (W_CV, vtc_ref)):
        vt = _dot(h, w_ref[:, c0:c0 + W_BR]).T
        vt_ref[...] = vt.reshape(W_BR // LANES, LANES, tm).astype(BF16)
    for c0 in range(0, OFF_BK, W_BR):
        if c0 != OFF_BQ:
            u_ref[:, c0:c0 + W_BR] = _dot(h, w_ref[:, c0:c0 + W_BR]).astype(BF16)


def _in_proj(x, pre_g, w_in, qn, kn, rope, layer, seq, tm):
    n = x.shape[0]
    nseq = seq // tm
    cos, slo, shi = rope
    vec = lambda width: pl.BlockSpec((None, 1, width), lambda t: (layer, 0, 0))
    tab = pl.BlockSpec((tm, LANES), lambda t: (t % nseq, 0))
    vt_spec = lambda heads: pl.BlockSpec((None, heads, LANES, tm), lambda t: (t // nseq, 0, 0, t % nseq))
    vt_shape = lambda heads: jax.ShapeDtypeStruct((n // seq, heads, LANES, seq), BF16)
    return pl.pallas_call(
        _in_proj_kernel,
        grid=(n // tm,),
        in_specs=[
            pl.BlockSpec((tm, D_MODEL), lambda t: (t, 0)),
            vec(D_MODEL),
            pl.BlockSpec((None, D_MODEL, IN_W), lambda t: (layer, 0, 0), pipeline_mode=pl.Buffered(1)),
            vec(LANES), vec(LANES), tab, tab, tab,
        ],
        out_specs=[pl.BlockSpec((tm, U_W), lambda t: (t, 0)), vt_spec(H_A), vt_spec(1), vt_spec(H_C // 2)],
        out_shape=[jax.ShapeDtypeStruct((n, U_W), BF16), vt_shape(H_A), vt_shape(1), vt_shape(H_C // 2)],
        compiler_params=pltpu.CompilerParams(dimension_semantics=("arbitrary",),
                                             vmem_limit_bytes=VMEM_LIMIT),
        name="in_proj",
    )(x, pre_g, w_in, qn, kn, cos, slo, shi)


def _chunk(c, tk):
    if isinstance(c, int):
        return slice(c * tk, (c + 1) * tk)
    return pl.ds(pl.multiple_of(c * tk, tk), tk)


class _T5Bias:
    def __init__(self, t_ref, cst_ref, base_w, base_r, per_q, nchunk):
        self.t_ref, self.cst_ref = t_ref, cst_ref
        self.base_w, self.base_r = base_w, base_r
        self.per_q, self.nchunk = per_q, nchunk

    def _tiled(self, n):
        return n <= self.per_q + 1

    def chunks(self, n):
        return (self.base_r + n - 1) % self.nchunk, (self.base_w + n - 1) % self.nchunk

    def add_tile(self, n, cw, mp, s):
        if not self._tiled(n):
            return s
        return s + self.t_ref[jnp.clip(cw - self.base_w, -2, self.per_q + 1) + 2, mp]

    def add_constant(self, n, cw, mp, m_c):
        if self._tiled(n):
            return m_c
        side = jnp.where(cw > self.base_w, self.t_ref.shape[0] - 1, 0)
        return m_c + self.t_ref[side, mp, 0:SUBLANES, :]

    def shifted_max(self, n, cr, mp, m):
        if self._tiled(n):
            return m
        return m - self.cst_ref[mp, (cr > self.base_r).astype(jnp.int32)]


def _two_stage_scratch(tq, seq, tk):
    return [pltpu.VMEM((2, seq // tk, tk, tq), F32), pltpu.VMEM((2, SUBLANES, tq), F32)] * 2


def _two_stage_step(qs, k_ref, vt_ref, bias, s_w, m_w, s_r, m_r):
    _, nchunk, tk, tq = s_w.shape
    slabs = tk // SUBLANES
    m_run, acc = [None] * 2, [None] * 2
    for n in range(nchunk):
        cr, cw = (n, n) if bias is None else bias.chunks(n)
        kc = k_ref[_chunk(cw, tk), :]
        vtc = _with_ones(vt_ref[:, _chunk(cr, tk)])
        for e in range(2):
            m_prev = m_r[e] if bias is None else bias.shifted_max(n, cr, e, m_r[e])
            p = jnp.exp2(s_r[e, cr].reshape(slabs, SUBLANES, tq) - m_prev[None])
            pv = _dot(vtc, p.reshape(tk, tq).astype(BF16))
            acc[e] = pv if n == 0 else acc[e] + pv
            s = _dot_nt(kc, qs[e])
            if bias is not None:
                s = bias.add_tile(n, cw, e, s)
            s_w[e, cw] = s
            m_c = jnp.max(s.reshape(slabs, SUBLANES, tq), axis=0)
            if bias is not None:
                m_c = bias.add_constant(n, cw, e, m_c)
            m_run[e] = m_c if n == 0 else jnp.maximum(m_run[e], m_c)
    outs = []
    for e in range(2):
        m_w[e] = jnp.broadcast_to(jnp.max(m_run[e], axis=0, keepdims=True), (SUBLANES, tq))
        outs.append((acc[e][:LANES] / acc[e][LANES:LANES + 1]).T)
    return outs


def _two_stage(step, scratch):
    s_a, m_a, s_b, m_b = scratch
    t = pl.program_id(0)

    @pl.when(t == 0)
    def _():
        s_b[...] = jnp.zeros_like(s_b)
        m_b[...] = jnp.zeros_like(m_b)

    @pl.when(t % 2 == 0)
    def _():
        step(s_a, m_a, s_b, m_b)

    @pl.when(t % 2 == 1)
    def _():
        step(s_b, m_b, s_a, m_a)


def _lane_halves(q):
    left = lax.broadcasted_iota(jnp.int32, q.shape, 1) < DH
    zero = jnp.zeros_like(q)
    return jnp.where(left, q, zero), jnp.where(left, zero, q)


def _unit_maps(b, heads, n_i):
    units = b * heads * n_i

    def unit(t, lag):
        un = jnp.clip(t - lag, 0, units - 1)
        return un // (heads * n_i), (un // n_i) % heads, un % n_i

    return units, unit


def _diff_attn_kernel(lam_ref, sg_ref, q_ref, k_ref, vt_ref, g_ref, t_ref, cst_ref, o_ref, *scratch,
                      lambda_init, n_i, units):
    t = pl.program_id(0)
    per_q = q_ref.shape[0] // t_ref.shape[2]
    i_w = jnp.minimum(t, units - 1) % n_i
    i_r = jnp.clip(t - 1, 0, units - 1) % n_i
    bias = _T5Bias(t_ref, cst_ref, i_w * per_q, i_r * per_q, per_q, k_ref.shape[0] // t_ref.shape[2])

    def step(*bufs):
        lp = lam_ref[...]
        lam = (jnp.exp(jnp.sum(lp[0:1] * lp[1:2], keepdims=True))
               - jnp.exp(jnp.sum(lp[2:3] * lp[3:4], keepdims=True)) + lambda_init)
        o0, o1 = _two_stage_step(_lane_halves(q_ref[...]), k_ref, vt_ref, bias, *bufs)
        o = o0 - lam * o1
        o = o * lax.rsqrt(jnp.mean(o * o, axis=-1, keepdims=True) + EPS) * sg_ref[...]
        o = o * (1.0 - lambda_init)
        o_ref[...] = (o * _silu(g_ref[...].astype(F32))).astype(BF16)

    _two_stage(step, scratch)


def _with_ones(vt):
    return jnp.concatenate([vt, jnp.ones((ONES_ROWS, vt.shape[1]), vt.dtype)], axis=0)


def _diff_attn(u, vt, lam_qk, subln_g, t5_tiles, layer, tq):
    b, seq, _ = u.shape
    ntile, _, tk, _ = t5_tiles.shape
    n_i = seq // tq
    units, unit = _unit_maps(b, H_A, n_i)
    lambda_init = 0.8 - 0.6 * math.exp(-0.3 * layer)
    far_bias = jnp.stack([t5_tiles[0, :, :SUBLANES], t5_tiles[-1, :, :SUBLANES]], axis=1)
    far_bias = far_bias.reshape(H_A, 2, 2, SUBLANES, tq)

    def row_blk(off, lag):
        def index(t):
            bi, h, i = unit(t, lag)
            return bi, i, off // LANES + h
        return pl.BlockSpec((None, tq, LANES), index)

    def seq_blk(off, lag):
        def index(t):
            bi, h, _ = unit(t, lag)
            return bi, 0, off // LANES + h
        return pl.BlockSpec((None, seq, LANES), index)

    return pl.pallas_call(
        functools.partial(_diff_attn_kernel, lambda_init=lambda_init, n_i=n_i, units=units),
        grid=(units + 1,),
        in_specs=[
            pl.BlockSpec((None, 4, DH), lambda t: (layer, 0, 0)),
            pl.BlockSpec((None, 1, LANES), lambda t: (layer, 0, 0)),
            row_blk(OFF_AQ, 0), seq_blk(OFF_AK, 0),
            pl.BlockSpec((None, None, LANES, seq), lambda t: unit(t, 1)[:2] + (0, 0)),
            row_blk(OFF_AG, 1),
            pl.BlockSpec((ntile, 2, tk, tq), lambda t: (0, unit(t, 0)[1], 0, 0)),
            pl.BlockSpec((None, 2, 2, SUBLANES, tq), lambda t: (unit(t, 1)[1], 0, 0, 0, 0)),
        ],
        out_specs=row_blk(0, 1),
        out_shape=jax.ShapeDtypeStruct((b, seq, W_BR), BF16),
        scratch_shapes=_two_stage_scratch(tq, seq, tk),
        compiler_params=pltpu.CompilerParams(dimension_semantics=("arbitrary",),
                                             vmem_limit_bytes=VMEM_LIMIT),
        name="diff_attn",
    )(lam_qk, subln_g, u, u, vt, u, t5_tiles, far_bias)


def _gqa_kernel(q_ref, k_ref, vt_ref, g_ref, o_ref, *scratch):
    def step(*bufs):
        o0, o1 = _two_stage_step(_lane_halves(q_ref[...]), k_ref, vt_ref, None, *bufs)
        left = lax.broadcasted_iota(jnp.int32, o0.shape, 1) < DH
        o = jnp.where(left, o0, o1)
        o_ref[...] = (o * _silu(g_ref[...].astype(F32))).astype(BF16)

    _two_stage(step, scratch)


def _gqa_attn(u, vt, tq):
    b, seq, _ = u.shape
    units, unit = _unit_maps(b, REP_B, seq // tq)

    def row_blk(off, lag):
        def index(t):
            bi, pr, i = unit(t, lag)
            return bi, i, off // LANES + pr
        return pl.BlockSpec((None, tq, LANES), index)

    def seq_blk(off, lag):
        return pl.BlockSpec((None, seq, LANES), lambda t: (unit(t, lag)[0], 0, off // LANES))

    return pl.pallas_call(
        _gqa_kernel,
        grid=(units + 1,),
        in_specs=[
            row_blk(OFF_BQ, 0), seq_blk(OFF_BK, 0),
            pl.BlockSpec((None, None, LANES, seq), lambda t: (unit(t, 1)[0], 0, 0, 0)),
            row_blk(OFF_BG, 1),
        ],
        out_specs=row_blk(0, 1),
        out_shape=jax.ShapeDtypeStruct((b, seq, W_BR), BF16),
        scratch_shapes=_two_stage_scratch(tq, seq, ATTN_TK),
        compiler_params=pltpu.CompilerParams(dimension_semantics=("arbitrary",),
                                             vmem_limit_bytes=VMEM_LIMIT),
        name="gqa_attn",
    )(u, u, vt, u)


def _na_block_type(bi, nblk):
    return jnp.where(bi < 2, bi, jnp.where(bi <= nblk - 3, 2, bi - (nblk - 5)))


def _na_kernel(q_ref, k_ref, vt_ref, g_ref, tab_ref, o_ref, *, rows):
    big = pl.program_id(1)
    nsub = q_ref.shape[0] // NA_QBLK
    nblk = rows // 2
    nkeys = NA_BAND * GRID_W
    left = lax.broadcasted_iota(jnp.int32, (NA_QBLK, LANES), 1) < DH

    def slices(sub, pair):
        bi = big * nsub + sub
        b0 = jnp.clip(2 * bi - NA_ROWS // 2, 0, rows - NA_BAND)
        ks = pl.ds(pl.multiple_of(b0 * GRID_W, 2 * GRID_W), nkeys)
        return (_na_block_type(bi, nblk), ks, slice(sub * NA_QBLK, (sub + 1) * NA_QBLK),
                slice(pair * LANES, (pair + 1) * LANES))

    def scores(sub, pair):
        typ, ks, rs, cs = slices(sub, pair)
        qq = jnp.concatenate(_lane_halves(q_ref[rs, cs]), axis=0)
        return _dot_nt(k_ref[ks, cs], qq) + tab_ref[typ, pair]

    def weights(s):
        return jnp.exp2(s - jnp.max(s, axis=0, keepdims=True)).astype(BF16)

    def finish(sub, pair, p):
        _, ks, rs, cs = slices(sub, pair)
        ot = _dot(_with_ones(vt_ref[pair, :, ks]), p)
        o = (ot[:LANES] / ot[LANES:LANES + 1]).T
        o = jnp.where(left, o[:NA_QBLK], o[NA_QBLK:])
        o_ref[rs, cs] = (o * _silu(g_ref[rs, cs].astype(F32))).astype(BF16)

    units = [(sub, pair) for sub in range(nsub) for pair in range(H_C // 2)]
    s_prev, p_prev = None, None
    for n in range(len(units) + 2):
        s_new = scores(*units[n]) if n < len(units) else None
        p_new = weights(s_prev) if s_prev is not None else None
        if p_prev is not None:
            finish(*units[n - 2], p_prev)
        s_prev, p_prev = s_new, p_new


def _na_attn(u, vt, na_tab, layer, tq):
    b, seq, _ = u.shape
    rows = seq // GRID_W
    assert (rows - NA_BAND) % 2 == 0
    row_blk = lambda off: pl.BlockSpec((None, tq, W_BR), lambda bi, i: (bi, i, off // W_BR))
    seq_blk = lambda off: pl.BlockSpec((None, seq, W_BR), lambda bi, i: (bi, 0, off // W_BR))
    return pl.pallas_call(
        functools.partial(_na_kernel, rows=rows),
        grid=(b, seq // tq),
        in_specs=[
            row_blk(OFF_CQ), seq_blk(OFF_CK),
            pl.BlockSpec((None,) + vt.shape[1:], lambda bi, i: (bi, 0, 0, 0)),
            row_blk(OFF_CG),
            pl.BlockSpec((None,) + na_tab.shape[1:], lambda bi, i: (layer, 0, 0, 0, 0),
                         pipeline_mode=pl.Buffered(1)),
        ],
        out_specs=pl.BlockSpec((None, tq, W_BR), lambda bi, i: (bi, i, 0)),
        out_shape=jax.ShapeDtypeStruct((b, seq, W_BR), BF16),
        compiler_params=pltpu.CompilerParams(dimension_semantics=("arbitrary", "arbitrary"),
                                             vmem_limit_bytes=VMEM_LIMIT),
        name="nbr_attn",
    )(u, u, vt, u, na_tab)


def _merge_kernel(x_ref, ya_ref, yb_ref, yc_ref, mg_ref, p_ref, wa_ref, wb_ref, wc_ref, wo_ref,
                  wpg_ref, wpe_ref, pg_ref, o_ref):
    def gate(k):
        return jax.nn.sigmoid(mg_ref[:, k * D_MODEL:(k + 1) * D_MODEL].astype(F32))

    m = gate(0) * _dot(ya_ref[...], wa_ref[...])
    m = m + gate(1) * _dot(yb_ref[...], wb_ref[...])
    m = m + gate(2) * _dot(yc_ref[...], wc_ref[...])
    r = _dot(m.astype(BF16), wo_ref[...])
    r = r * lax.rsqrt(jnp.mean(r * r, axis=-1, keepdims=True) + EPS) * pg_ref[...]
    x = x_ref[...] + r
    emb = _dot(p_ref[...].astype(BF16), wpe_ref[...])
    o_ref[...] = x + jax.nn.sigmoid(_dot(x.astype(BF16), wpg_ref[...])) * emb


def _merge(x, ya, yb, yc, u, p, wa, wb, wc, wo, wpg, wpe, post_g, layer, tm):
    n = x.shape[0]
    tok = lambda width: pl.BlockSpec((tm, width), lambda t: (t, 0))
    wgt = lambda r, c: pl.BlockSpec((None, r, c), lambda t: (layer, 0, 0))
    return pl.pallas_call(
        _merge_kernel,
        grid=(n // tm,),
        in_specs=[
            tok(D_MODEL), tok(W_BR), tok(W_BR), tok(W_BR),
            pl.BlockSpec((tm, 3 * D_MODEL), lambda t: (t, OFF_MG // (3 * D_MODEL))),
            pl.BlockSpec((None, tm, D_PLE), lambda t: (layer, t, 0)),
            wgt(W_BR, D_MODEL), wgt(W_BR, D_MODEL), wgt(W_BR, D_MODEL),
            wgt(D_MODEL, D_MODEL), wgt(D_MODEL, D_MODEL), wgt(D_PLE, D_MODEL),
            wgt(1, D_MODEL),
        ],
        out_specs=tok(D_MODEL),
        out_shape=jax.ShapeDtypeStruct((n, D_MODEL), F32),
        compiler_params=pltpu.CompilerParams(dimension_semantics=("arbitrary",),
                                             vmem_limit_bytes=VMEM_LIMIT),
        name="merge",
    )(x, ya, yb, yc, u, p, wa, wb, wc, wo, wpg, wpe, post_g)


def _t5_bucket(rel):
    nb = T5_BUCKETS // 2
    ret = jnp.where(rel > 0, nb, 0)
    n = jnp.abs(rel)
    max_exact = nb // 2
    nf = jnp.maximum(n, 1).astype(F32)
    large = max_exact + (jnp.log(nf / max_exact) / math.log(T5_MAX_DIST / max_exact)
                         * (nb - max_exact)).astype(jnp.int32)
    large = jnp.minimum(large, nb - 1)
    return ret + jnp.where(n < max_exact, n, large)


def _t5_tiles(t5_table, seq, tq, tk):
    assert tk >= T5_MAX_DIST and tq % tk == 0
    rel = jnp.arange(-(seq - 1), seq)
    rel_bias = t5_table[_t5_bucket(rel)].astype(F32).T * LOG2E
    big = tq + (tq // tk + 2) * tk
    padded = jnp.pad(rel_bias, ((0, 0), (big, big)), mode="edge")
    period = tq + tk
    tiles = []
    for e in range(-2, tq // tk + 2):
        base = seq - 1 + big + e * tk
        rev = padded[:, base - tq:base + tk][:, ::-1]
        w = jnp.concatenate([rev[:, tk - 1:], rev[:, :tk - 1]], axis=1)
        toe = jnp.tile(w, (1, tk))[:, :tk * (period - 1)].reshape(-1, tk, period - 1)
        tiles.append(toe[:, :, :tq])
    return jnp.stack(tiles)


def _rope_tables(seq):
    t = jnp.arange(seq)
    row = (t // GRID_W).astype(F32)
    col = (t % GRID_W).astype(F32)
    n_freq = DH // 4
    inv = ROPE_THETA ** (-jnp.arange(n_freq, dtype=F32) / n_freq)
    ang = jnp.concatenate([row[:, None] * inv] * 2 + [col[:, None] * inv] * 2, axis=1)
    cos, sin = jnp.cos(ang), jnp.sin(ang)
    first_half = (jnp.arange(DH) % (2 * n_freq)) < n_freq
    sin_lo = jnp.where(first_half, -sin, 0.0)
    sin_hi = jnp.where(first_half, 0.0, sin)
    two = lambda a: jnp.concatenate([a, a], axis=1)
    return two(cos), two(sin_lo), two(sin_hi)


def _na_tables(rpb, seq):
    rows = seq // GRID_W
    nblk = rows // 2
    assert rows >= NA_BAND and nblk >= 5
    wr = min(NA_ROWS, rows)
    depth = rpb.shape[0]
    ncol = 2 * NA_COLS - 1
    bis = jnp.array([0, 1, 2, nblk - 2, nblk - 1])[:, None, None, None, None]
    qr = 2 * bis + jnp.arange(2)[None, :, None, None, None]
    qc = jnp.arange(GRID_W)[None, None, :, None, None]
    kr = (jnp.clip(2 * bis - NA_ROWS // 2, 0, rows - NA_BAND)
          + jnp.arange(NA_BAND)[None, None, None, :, None])
    kc = jnp.arange(GRID_W)[None, None, None, None, :]
    sr = jnp.clip(qr - wr // 2, 0, rows - wr)
    sc = jnp.clip(qc - NA_COLS // 2, 0, GRID_W - NA_COLS)
    mask = (kr >= sr) & (kr < sr + wr) & (kc >= sc) & (kc < sc + NA_COLS)
    ri = jnp.clip(kr - qr + NA_ROWS - 1, 0, 2 * NA_ROWS - 2)[:, :, 0, :, 0]
    ci = jnp.clip(kc - qc + NA_COLS - 1, 0, ncol - 1)[0, 0, :, 0, :]
    by_row = rpb.astype(F32)[:, :, ri, :]
    by_row = by_row.reshape(depth, H_C // 2, 2, 5, 2, NA_BAND, ncol)
    by_row = jnp.transpose(by_row, (0, 3, 1, 5, 2, 4, 6))
    ea = jnp.arange(4)
    same_ea = (ea[None, :, None, None, None] == ea[None, None, None, :, None])
    hit = (jnp.arange(ncol)[None, None, :, None, None] == ci.T[:, None, None, None, :])
    onehot = (same_ea & hit).astype(F32).reshape(GRID_W, 4 * ncol, 2 * NA_QBLK)
    bias = jnp.einsum("rj,kjc->rkc", by_row.reshape(-1, 4 * ncol), onehot, precision=lax.Precision.HIGHEST)
    bias = bias.reshape(depth, 5, H_C // 2, NA_BAND * GRID_W, 2 * NA_QBLK)
    mask = jnp.transpose(mask, (0, 3, 4, 1, 2)).reshape(5, NA_BAND * GRID_W, NA_QBLK)
    mask = jnp.concatenate([mask, mask], axis=-1)
    return jnp.where(mask[None, :, None], bias * LOG2E, -jnp.inf)


def _prepare(t5_table, pre_norm_g, w_in, subln_g, q_norm_g, k_norm_g, na_rpb, w_branch_a, w_branch_b,
             w_branch_c, w_out, post_norm_g, w_ple_proj, w_ple_gate, seq, tq):
    depth = w_in.shape[0]
    two = lambda a: jnp.concatenate([a, a], axis=-1).reshape(depth, 1, LANES)
    return dict(
        w_in=_permute_in_proj(w_in).astype(BF16),
        pre_g=pre_norm_g.reshape(depth, 1, D_MODEL),
        qn=two(q_norm_g), kn=two(k_norm_g),
        subln=subln_g.reshape(depth, 1, LANES),
        wa=w_branch_a.astype(BF16), wb=_pair_major_heads(w_branch_b, 1).astype(BF16), wc=w_branch_c.astype(BF16),
        wo=w_out.astype(BF16), wpg=w_ple_gate.astype(BF16), wpe=w_ple_proj.astype(BF16),
        post_g=post_norm_g.reshape(depth, 1, D_MODEL),
        t5=_t5_tiles(t5_table, seq, tq, T5_TK), rope=_rope_tables(seq), na=_na_tables(na_rpb, seq),
    )


def _trunk(x, p, lambda_qk, w, *, tm, tq_a, tq_b, tq_c):
    b, seq, _ = x.shape
    depth = p.shape[0]
    n = b * seq
    xf = x.reshape(n, D_MODEL)
    pf = p.reshape(depth, n, D_PLE)
    for layer in range(depth):
        u, vta, vtb, vtc = _in_proj(xf, w["pre_g"], w["w_in"], w["qn"], w["kn"], w["rope"], layer, seq, tm)
        u3 = u.reshape(b, seq, U_W)
        ya = _diff_attn(u3, vta, lambda_qk, w["subln"], w["t5"], layer, tq_a)
        yb = _gqa_attn(u3, vtb, tq_b)
        yc = _na_attn(u3, vtc, w["na"], layer, tq_c)
        xf = _merge(xf, ya.reshape(n, W_BR), yb.reshape(n, W_BR), yc.reshape(n, W_BR), u, pf,
                    w["wa"], w["wb"], w["wc"], w["wo"], w["wpg"], w["wpe"], w["post_g"], layer, tm)
    return xf.reshape(b, seq, D_MODEL)


TILES = dict(tm=512, tq_a=512, tq_b=512, tq_c=1024)


def kernel(x_prompt, x_sample, p_prompt, p_sample, t5_table, pre_norm_g, w_in, lambda_qk, subln_g, q_norm_g,
           k_norm_g, na_rpb, w_branch_a, w_branch_b, w_branch_c, w_out, post_norm_g, w_ple_proj, w_ple_gate):
    assert x_prompt.shape[1] == x_sample.shape[1]
    w = _prepare(t5_table, pre_norm_g, w_in, subln_g, q_norm_g, k_norm_g, na_rpb, w_branch_a, w_branch_b,
                 w_branch_c, w_out, post_norm_g, w_ple_proj, w_ple_gate, x_prompt.shape[1], TILES["tq_a"])
    y_prompt = _trunk(x_prompt, p_prompt, lambda_qk, w, **TILES)
    y_sample = _trunk(x_sample, p_sample, lambda_qk, w, **TILES)
    return (y_prompt, y_sample)
```

```python
import functools
import math

import jax
import jax.numpy as jnp
from jax import lax
from jax.experimental import pallas as pl
from jax.experimental.pallas import tpu as pltpu

F32 = jnp.float32
BF16 = jnp.bfloat16

D_MODEL = 1024
D_PLE = 256
EPS = 1e-6
GRID_W = 64
LANES = 128
SUBLANES = 8
ONES_ROWS = 16
DH = 64
H_A = 4
H_B = 8
KV_B = 2
REP_B = H_B // KV_B
H_C = 8
W_BR = 512
NA_ROWS = 8
NA_COLS = 16
NA_BAND = 10
NA_QBLK = 2 * GRID_W
T5_BUCKETS = 32
T5_MAX_DIST = 128
ATTN_TK = 256
T5_TK = ATTN_TK
ROPE_THETA = 10000.0
LOG2E = math.log2(math.e)
QK_SCALE = DH ** -0.5 * LOG2E

OFF_MG = 0
OFF_AQ, OFF_AK, OFF_AG = 3072, 3584, 4096
OFF_BQ, OFF_BG = 4608, 5120
OFF_CQ, OFF_CK, OFF_CG = 5632, 6144, 6656
OFF_BK = 7168
U_W = OFF_BK + LANES
W_AV, W_CV, W_BKV = 7168, 7680, 8192
IN_W = 8448

VMEM_LIMIT = 56 * 1024 * 1024


def _pair_major_heads(w, axis):
    shp = w.shape
    w = w.reshape(shp[:axis] + (KV_B, REP_B, DH) + shp[axis + 1:])
    return jnp.swapaxes(w, axis, axis + 1).reshape(shp)


def _permute_in_proj(w_in):
    sizes = dict(aq=512, ak=512, av=512, ag=512, bq=512, bk=128, bv=128, bg=512, cq=512, ck=512, cv=512,
                 cg=512, mg=3072)
    seg, off = {}, 0
    for name, size in sizes.items():
        seg[name] = w_in[:, :, off:off + size]
        off += size
    seg["aq"] = seg["aq"] * QK_SCALE
    seg["cq"] = seg["cq"] * QK_SCALE
    seg["bq"] = _pair_major_heads(seg["bq"], 2)
    seg["bg"] = _pair_major_heads(seg["bg"], 2)
    order = ("mg", "aq", "ak", "ag", "bq", "bg", "cq", "ck", "cg", "av", "cv", "bk", "bv")
    return jnp.concatenate([seg[n] for n in order], axis=2)


def _silu(x):
    return x * jax.nn.sigmoid(x)


def _dot_nt(a, b):
    return lax.dot_general(a, b, (((1,), (1,)), ((), ())), preferred_element_type=F32)


def _dot(a, b):
    return jnp.dot(a, b, preferred_element_type=F32)


def _pair_norm_rope(a, gain, cos, sin_lo, sin_hi, left):
    sq = a * a
    s_l = jnp.sum(jnp.where(left, sq, 0.0), axis=-1, keepdims=True)
    s_r = jnp.sum(jnp.where(left, 0.0, sq), axis=-1, keepdims=True)
    ms = jnp.where(left, s_l, s_r) * (1.0 / DH)
    y = a * lax.rsqrt(ms + EPS) * gain
    return y * cos + pltpu.roll(y, 112, 1) * sin_lo + pltpu.roll(y, 16, 1) * sin_hi


def _in_proj_kernel(x_ref, g_ref, w_ref, qn_ref, kn_ref, cos_ref, slo_ref, shi_ref,
                    u_ref, vta_ref, vtb_ref, vtc_ref):
    x = x_ref[...]
    ms = jnp.mean(x * x, axis=-1, keepdims=True)
    h = (x * lax.rsqrt(ms + EPS) * g_ref[...]).astype(BF16)
    tm = x.shape[0]
    left = lax.broadcasted_iota(jnp.int32, (tm, LANES), 1) < DH
    cos, slo, shi = cos_ref[...], slo_ref[...], shi_ref[...]
    acc = _dot(h, w_ref[:, W_BKV:IN_W])
    bk = _pair_norm_rope(acc[:, :LANES], kn_ref[...], cos, slo, shi, left)
    u_ref[:, OFF_BK:U_W] = bk.astype(BF16)
    vtb_ref[0] = acc[:, LANES:].T.astype(BF16)
    acc = _dot(h, w_ref[:, OFF_BQ:OFF_BQ + W_BR])
    for b in range(W_BR // LANES):
        blk = _pair_norm_rope(acc[:, b * LANES:(b + 1) * LANES], qn_ref[...], cos, slo, shi, left)
        u_ref[:, OFF_BQ + b * LANES:OFF_BQ + (b + 1) * LANES] = (blk * QK_SCALE).astype(BF16)
    for c0, vt_ref in ((W_AV, vta_ref), (W_CV, vtc_ref)):
        vt = _dot(h, w_ref[:, c0:c0 + W_BR]).T
        vt_ref[...] = vt.reshape(W_BR // LANES, LANES, tm).astype(BF16)
    for c0 in range(0, OFF_BK, W_BR):
        if c0 != OFF_BQ:
            u_ref[:, c0:c0 + W_BR] = _dot(h, w_ref[:, c0:c0 + W_BR]).astype(BF16)


def _in_proj(x, pre_g, w_in, qn, kn, rope, layer, seq, tm):
    n = x.shape[0]
    nseq = seq // tm
    cos, slo, shi = rope
    vec = lambda width: pl.BlockSpec((None, 1, width), lambda t: (layer, 0, 0))
    tab = pl.BlockSpec((tm, LANES), lambda t: (t % nseq, 0))
    vt_spec = lambda heads: pl.BlockSpec((None, heads, LANES, tm), lambda t: (t // nseq, 0, 0, t % nseq))
    vt_shape = lambda heads: jax.ShapeDtypeStruct((n // seq, heads, LANES, seq), BF16)
    return pl.pallas_call(
        _in_proj_kernel,
        grid=(n // tm,),
        in_specs=[
            pl.BlockSpec((tm, D_MODEL), lambda t: (t, 0)),
            vec(D_MODEL),
            pl.BlockSpec((None, D_MODEL, IN_W), lambda t: (layer, 0, 0), pipeline_mode=pl.Buffered(1)),
            vec(LANES), vec(LANES), tab, tab, tab,
        ],
        out_specs=[pl.BlockSpec((tm, U_W), lambda t: (t, 0)), vt_spec(H_A), vt_spec(1), vt_spec(H_C // 2)],
        out_shape=[jax.ShapeDtypeStruct((n, U_W), BF16), vt_shape(H_A), vt_shape(1), vt_shape(H_C // 2)],
        compiler_params=pltpu.CompilerParams(dimension_semantics=("arbitrary",),
                                             vmem_limit_bytes=VMEM_LIMIT),
        name="in_proj",
    )(x, pre_g, w_in, qn, kn, cos, slo, shi)


def _chunk(c, tk):
    if isinstance(c, int):
        return slice(c * tk, (c + 1) * tk)
    return pl.ds(pl.multiple_of(c * tk, tk), tk)


class _T5Bias:
    def __init__(self, t_ref, cst_ref, base_w, base_r, per_q, nchunk):
        self.t_ref, self.cst_ref = t_ref, cst_ref
        self.base_w, self.base_r = base_w, base_r
        self.per_q, self.nchunk = per_q, nchunk

    def _tiled(self, n):
        return n <= self.per_q + 1

    def chunks(self, n):
        return (self.base_r + n - 1) % self.nchunk, (self.base_w + n - 1) % self.nchunk

    def add_tile(self, n, cw, mp, s):
        if not self._tiled(n):
            return s
        return s + self.t_ref[jnp.clip(cw - self.base_w, -2, self.per_q + 1) + 2, mp]

    def add_constant(self, n, cw, mp, m_c):
        if self._tiled(n):
            return m_c
        side = jnp.where(cw > self.base_w, self.t_ref.shape[0] - 1, 0)
        return m_c + self.t_ref[side, mp, 0:SUBLANES, :]

    def shifted_max(self, n, cr, mp, m):
        if self._tiled(n):
            return m
        return m - self.cst_ref[mp, (cr > self.base_r).astype(jnp.int32)]


def _two_stage_scratch(tq, seq, tk):
    return [pltpu.VMEM((2, seq // tk, tk, tq), F32), pltpu.VMEM((2, SUBLANES, tq), F32)] * 2


def _two_stage_step(qs, k_ref, vt_ref, bias, s_w, m_w, s_r, m_r):
    _, nchunk, tk, tq = s_w.shape
    slabs = tk // SUBLANES
    m_run, acc = [None] * 2, [None] * 2
    for n in range(nchunk):
        cr, cw = (n, n) if bias is None else bias.chunks(n)
        kc = k_ref[_chunk(cw, tk), :]
        vtc = _with_ones(vt_ref[:, _chunk(cr, tk)])
        for e in range(2):
            m_prev = m_r[e] if bias is None else bias.shifted_max(n, cr, e, m_r[e])
            p = jnp.exp2(s_r[e, cr].reshape(slabs, SUBLANES, tq) - m_prev[None])
            pv = _dot(vtc, p.reshape(tk, tq).astype(BF16))
            acc[e] = pv if n == 0 else acc[e] + pv
            s = _dot_nt(kc, qs[e])
            if bias is not None:
                s = bias.add_tile(n, cw, e, s)
            s_w[e, cw] = s
            m_c = jnp.max(s.reshape(slabs, SUBLANES, tq), axis=0)
            if bias is not None:
                m_c = bias.add_constant(n, cw, e, m_c)
            m_run[e] = m_c if n == 0 else jnp.maximum(m_run[e], m_c)
    outs = []
    for e in range(2):
        m_w[e] = jnp.broadcast_to(jnp.max(m_run[e], axis=0, keepdims=True), (SUBLANES, tq))
        outs.append((acc[e][:LANES] / acc[e][LANES:LANES + 1]).T)
    return outs


def _two_stage(step, scratch):
    s_a, m_a, s_b, m_b = scratch
    t = pl.program_id(0)

    @pl.when(t == 0)
    def _():
        s_b[...] = jnp.zeros_like(s_b)
        m_b[...] = jnp.zeros_like(m_b)

    @pl.when(t % 2 == 0)
    def _():
        step(s_a, m_a, s_b, m_b)

    @pl.when(t % 2 == 1)
    def _():
        step(s_b, m_b, s_a, m_a)


def _lane_halves(q):
    left = lax.broadcasted_iota(jnp.int32, q.shape, 1) < DH
    zero = jnp.zeros_like(q)
    return jnp.where(left, q, zero), jnp.where(left, zero, q)


def _unit_maps(b, heads, n_i, head_major=False):
    units = b * heads * n_i

    def unit(t, lag):
        un = jnp.clip(t - lag, 0, units - 1)
        if head_major:
            return (un // n_i) % b, un // (b * n_i), un % n_i
        return un // (heads * n_i), (un // n_i) % heads, un % n_i

    return units, unit


def _diff_attn_kernel(lam_ref, sg_ref, q_ref, k_ref, vt_ref, g_ref, t_ref, cst_ref, o_ref, *scratch,
                      lambda_init, n_i, units):
    t = pl.program_id(0)
    per_q = q_ref.shape[0] // t_ref.shape[2]
    i_w = jnp.minimum(t, units - 1) % n_i
    i_r = jnp.clip(t - 1, 0, units - 1) % n_i
    bias = _T5Bias(t_ref, cst_ref, i_w * per_q, i_r * per_q, per_q, k_ref.shape[0] // t_ref.shape[2])

    def step(*bufs):
        lp = lam_ref[...]
        lam = (jnp.exp(jnp.sum(lp[0:1] * lp[1:2], keepdims=True))
               - jnp.exp(jnp.sum(lp[2:3] * lp[3:4], keepdims=True)) + lambda_init)
        o0, o1 = _two_stage_step(_lane_halves(q_ref[...]), k_ref, vt_ref, bias, *bufs)
        o = o0 - lam * o1
        o = o * lax.rsqrt(jnp.mean(o * o, axis=-1, keepdims=True) + EPS) * sg_ref[...]
        o = o * (1.0 - lambda_init)
        o_ref[...] = (o * _silu(g_ref[...].astype(F32))).astype(BF16)

    _two_stage(step, scratch)


def _with_ones(vt):
    return jnp.concatenate([vt, jnp.ones((ONES_ROWS, vt.shape[1]), vt.dtype)], axis=0)


def _diff_attn(u, vt, lam_qk, subln_g, t5_tiles, layer, tq):
    b, seq, _ = u.shape
    ntile, _, tk, _ = t5_tiles.shape
    n_i = seq // tq
    units, unit = _unit_maps(b, H_A, n_i, head_major=True)
    lambda_init = 0.8 - 0.6 * math.exp(-0.3 * layer)
    far_bias = jnp.stack([t5_tiles[0, :, :SUBLANES], t5_tiles[-1, :, :SUBLANES]], axis=1)
    far_bias = far_bias.reshape(H_A, 2, 2, SUBLANES, tq)

    def row_blk(off, lag):
        def index(t):
            bi, h, i = unit(t, lag)
            return bi, i, off // LANES + h
        return pl.BlockSpec((None, tq, LANES), index)

    def seq_blk(off, lag):
        def index(t):
            bi, h, _ = unit(t, lag)
            return bi, 0, off // LANES + h
        return pl.BlockSpec((None, seq, LANES), index)

    return pl.pallas_call(
        functools.partial(_diff_attn_kernel, lambda_init=lambda_init, n_i=n_i, units=units),
        grid=(units + 1,),
        in_specs=[
            pl.BlockSpec((None, 4, DH), lambda t: (layer, 0, 0)),
            pl.BlockSpec((None, 1, LANES), lambda t: (layer, 0, 0)),
            row_blk(OFF_AQ, 0), seq_blk(OFF_AK, 0),
            pl.BlockSpec((None, None, LANES, seq), lambda t: unit(t, 1)[:2] + (0, 0)),
            row_blk(OFF_AG, 1),
            pl.BlockSpec((ntile, 2, tk, tq), lambda t: (0, unit(t, 0)[1], 0, 0)),
            pl.BlockSpec((None, 2, 2, SUBLANES, tq), lambda t: (unit(t, 1)[1], 0, 0, 0, 0)),
        ],
        out_specs=row_blk(0, 1),
        out_shape=jax.ShapeDtypeStruct((b, seq, W_BR), BF16),
        scratch_shapes=_two_stage_scratch(tq, seq, tk),
        compiler_params=pltpu.CompilerParams(dimension_semantics=("arbitrary",),
                                             vmem_limit_bytes=VMEM_LIMIT),
        name="diff_attn",
    )(lam_qk, subln_g, u, u, vt, u, t5_tiles, far_bias)


def _gqa_kernel(q_ref, k_ref, vt_ref, g_ref, o_ref, *scratch):
    def step(*bufs):
        o0, o1 = _two_stage_step(_lane_halves(q_ref[...]), k_ref, vt_ref, None, *bufs)
        left = lax.broadcasted_iota(jnp.int32, o0.shape, 1) < DH
        o = jnp.where(left, o0, o1)
        o_ref[...] = (o * _silu(g_ref[...].astype(F32))).astype(BF16)

    _two_stage(step, scratch)


def _gqa_attn(u, vt, tq):
    b, seq, _ = u.shape
    units, unit = _unit_maps(b, REP_B, seq // tq)

    def row_blk(off, lag):
        def index(t):
            bi, pr, i = unit(t, lag)
            return bi, i, off // LANES + pr
        return pl.BlockSpec((None, tq, LANES), index)

    def seq_blk(off, lag):
        return pl.BlockSpec((None, seq, LANES), lambda t: (unit(t, lag)[0], 0, off // LANES))

    return pl.pallas_call(
        _gqa_kernel,
        grid=(units + 1,),
        in_specs=[
            row_blk(OFF_BQ, 0), seq_blk(OFF_BK, 0),
            pl.BlockSpec((None, None, LANES, seq), lambda t: (unit(t, 1)[0], 0, 0, 0)),
            row_blk(OFF_BG, 1),
        ],
        out_specs=row_blk(0, 1),
        out_shape=jax.ShapeDtypeStruct((b, seq, W_BR), BF16),
        scratch_shapes=_two_stage_scratch(tq, seq, ATTN_TK),
        compiler_params=pltpu.CompilerParams(dimension_semantics=("arbitrary",),
                                             vmem_limit_bytes=VMEM_LIMIT),
        name="gqa_attn",
    )(u, u, vt, u)


def _na_block_type(bi, nblk):
    return jnp.where(bi < 2, bi, jnp.where(bi <= nblk - 3, 2, bi - (nblk - 5)))


def _na_kernel(q_ref, k_ref, vt_ref, g_ref, tab_ref, o_ref, *, rows):
    big = pl.program_id(1)
    nsub = q_ref.shape[0] // NA_QBLK
    nblk = rows // 2
    nkeys = NA_BAND * GRID_W
    left = lax.broadcasted_iota(jnp.int32, (NA_QBLK, LANES), 1) < DH

    def slices(sub, pair):
        bi = big * nsub + sub
        b0 = jnp.clip(2 * bi - NA_ROWS // 2, 0, rows - NA_BAND)
        ks = pl.ds(pl.multiple_of(b0 * GRID_W, 2 * GRID_W), nkeys)
        return (_na_block_type(bi, nblk), ks, slice(sub * NA_QBLK, (sub + 1) * NA_QBLK),
                slice(pair * LANES, (pair + 1) * LANES))

    def scores(sub, pair):
        typ, ks, rs, cs = slices(sub, pair)
        qq = jnp.concatenate(_lane_halves(q_ref[rs, cs]), axis=0)
        return _dot_nt(k_ref[ks, cs], qq) + tab_ref[typ, pair]

    def weights(s):
        return jnp.exp2(s - jnp.max(s, axis=0, keepdims=True)).astype(BF16)

    def finish(sub, pair, p):
        _, ks, rs, cs = slices(sub, pair)
        ot = _dot(_with_ones(vt_ref[pair, :, ks]), p)
        o = (ot[:LANES] / ot[LANES:LANES + 1]).T
        o = jnp.where(left, o[:NA_QBLK], o[NA_QBLK:])
        o_ref[rs, cs] = (o * _silu(g_ref[rs, cs].astype(F32))).astype(BF16)

    units = [(sub, pair) for sub in range(nsub) for pair in range(H_C // 2)]
    s_prev, p_prev = None, None
    for n in range(len(units) + 2):
        s_new = scores(*units[n]) if n < len(units) else None
        p_new = weights(s_prev) if s_prev is not None else None
        if p_prev is not None:
            finish(*units[n - 2], p_prev)
        s_prev, p_prev = s_new, p_new


def _na_attn(u, vt, na_tab, layer, tq):
    b, seq, _ = u.shape
    rows = seq // GRID_W
    assert (rows - NA_BAND) % 2 == 0
    row_blk = lambda off: pl.BlockSpec((None, tq, W_BR), lambda bi, i: (bi, i, off // W_BR))
    seq_blk = lambda off: pl.BlockSpec((None, seq, W_BR), lambda bi, i: (bi, 0, off // W_BR))
    return pl.pallas_call(
        functools.partial(_na_kernel, rows=rows),
        grid=(b, seq // tq),
        in_specs=[
            row_blk(OFF_CQ), seq_blk(OFF_CK),
            pl.BlockSpec((None,) + vt.shape[1:], lambda bi, i: (bi, 0, 0, 0)),
            row_blk(OFF_CG),
            pl.BlockSpec((None,) + na_tab.shape[1:], lambda bi, i: (layer, 0, 0, 0, 0),
                         pipeline_mode=pl.Buffered(1)),
        ],
        out_specs=pl.BlockSpec((None, tq, W_BR), lambda bi, i: (bi, i, 0)),
        out_shape=jax.ShapeDtypeStruct((b, seq, W_BR), BF16),
        compiler_params=pltpu.CompilerParams(dimension_semantics=("arbitrary", "arbitrary"),
                                             vmem_limit_bytes=VMEM_LIMIT),
        name="nbr_attn",
    )(u, u, vt, u, na_tab)


def _merge_kernel(x_ref, ya_ref, yb_ref, yc_ref, mg_ref, p_ref, wa_ref, wb_ref, wc_ref, wo_ref,
                  wpg_ref, wpe_ref, pg_ref, o_ref):
    def gate(k):
        return jax.nn.sigmoid(mg_ref[:, k * D_MODEL:(k + 1) * D_MODEL].astype(F32))

    m = gate(0) * _dot(ya_ref[...], wa_ref[...])
    m = m + gate(1) * _dot(yb_ref[...], wb_ref[...])
    m = m + gate(2) * _dot(yc_ref[...], wc_ref[...])
    r = _dot(m.astype(BF16), wo_ref[...])
    r = r * lax.rsqrt(jnp.mean(r * r, axis=-1, keepdims=True) + EPS) * pg_ref[...]
    x = x_ref[...] + r
    emb = _dot(p_ref[...].astype(BF16), wpe_ref[...])
    o_ref[...] = x + jax.nn.sigmoid(_dot(x.astype(BF16), wpg_ref[...])) * emb


def _merge(x, ya, yb, yc, u, p, wa, wb, wc, wo, wpg, wpe, post_g, layer, tm):
    n = x.shape[0]
    tok = lambda width: pl.BlockSpec((tm, width), lambda t: (t, 0))
    wgt = lambda r, c: pl.BlockSpec((None, r, c), lambda t: (layer, 0, 0))
    return pl.pallas_call(
        _merge_kernel,
        grid=(n // tm,),
        in_specs=[
            tok(D_MODEL), tok(W_BR), tok(W_BR), tok(W_BR),
            pl.BlockSpec((tm, 3 * D_MODEL), lambda t: (t, OFF_MG // (3 * D_MODEL))),
            pl.BlockSpec((None, tm, D_PLE), lambda t: (layer, t, 0)),
            wgt(W_BR, D_MODEL), wgt(W_BR, D_MODEL), wgt(W_BR, D_MODEL),
            wgt(D_MODEL, D_MODEL), wgt(D_MODEL, D_MODEL), wgt(D_PLE, D_MODEL),
            wgt(1, D_MODEL),
        ],
        out_specs=tok(D_MODEL),
        out_shape=jax.ShapeDtypeStruct((n, D_MODEL), F32),
        compiler_params=pltpu.CompilerParams(dimension_semantics=("arbitrary",),
                                             vmem_limit_bytes=VMEM_LIMIT),
        name="merge",
    )(x, ya, yb, yc, u, p, wa, wb, wc, wo, wpg, wpe, post_g)


def _t5_bucket(rel):
    nb = T5_BUCKETS // 2
    ret = jnp.where(rel > 0, nb, 0)
    n = jnp.abs(rel)
    max_exact = nb // 2
    nf = jnp.maximum(n, 1).astype(F32)
    large = max_exact + (jnp.log(nf / max_exact) / math.log(T5_MAX_DIST / max_exact)
                         * (nb - max_exact)).astype(jnp.int32)
    large = jnp.minimum(large, nb - 1)
    return ret + jnp.where(n < max_exact, n, large)


def _t5_tiles(t5_table, seq, tq, tk):
    assert tk >= T5_MAX_DIST and tq % tk == 0
    rel = jnp.arange(-(seq - 1), seq)
    rel_bias = t5_table[_t5_bucket(rel)].astype(F32).T * LOG2E
    big = tq + (tq // tk + 2) * tk
    padded = jnp.pad(rel_bias, ((0, 0), (big, big)), mode="edge")
    period = tq + tk
    tiles = []
    for e in range(-2, tq // tk + 2):
        base = seq - 1 + big + e * tk
        rev = padded[:, base - tq:base + tk][:, ::-1]
        w = jnp.concatenate([rev[:, tk - 1:], rev[:, :tk - 1]], axis=1)
        toe = jnp.tile(w, (1, tk))[:, :tk * (period - 1)].reshape(-1, tk, period - 1)
        tiles.append(toe[:, :, :tq])
    return jnp.stack(tiles)


def _rope_tables(seq):
    t = jnp.arange(seq)
    row = (t // GRID_W).astype(F32)
    col = (t % GRID_W).astype(F32)
    n_freq = DH // 4
    inv = ROPE_THETA ** (-jnp.arange(n_freq, dtype=F32) / n_freq)
    ang = jnp.concatenate([row[:, None] * inv] * 2 + [col[:, None] * inv] * 2, axis=1)
    cos, sin = jnp.cos(ang), jnp.sin(ang)
    first_half = (jnp.arange(DH) % (2 * n_freq)) < n_freq
    sin_lo = jnp.where(first_half, -sin, 0.0)
    sin_hi = jnp.where(first_half, 0.0, sin)
    two = lambda a: jnp.concatenate([a, a], axis=1)
    return two(cos), two(sin_lo), two(sin_hi)


def _na_tables(rpb, seq):
    rows = seq // GRID_W
    nblk = rows // 2
    assert rows >= NA_BAND and nblk >= 5
    wr = min(NA_ROWS, rows)
    depth = rpb.shape[0]
    ncol = 2 * NA_COLS - 1
    bis = jnp.array([0, 1, 2, nblk - 2, nblk - 1])[:, None, None, None, None]
    qr = 2 * bis + jnp.arange(2)[None, :, None, None, None]
    qc = jnp.arange(GRID_W)[None, None, :, None, None]
    kr = (jnp.clip(2 * bis - NA_ROWS // 2, 0, rows - NA_BAND)
          + jnp.arange(NA_BAND)[None, None, None, :, None])
    kc = jnp.arange(GRID_W)[None, None, None, None, :]
    sr = jnp.clip(qr - wr // 2, 0, rows - wr)
    sc = jnp.clip(qc - NA_COLS // 2, 0, GRID_W - NA_COLS)
    mask = (kr >= sr) & (kr < sr + wr) & (kc >= sc) & (kc < sc + NA_COLS)
    ri = jnp.clip(kr - qr + NA_ROWS - 1, 0, 2 * NA_ROWS - 2)[:, :, 0, :, 0]
    ci = jnp.clip(kc - qc + NA_COLS - 1, 0, ncol - 1)[0, 0, :, 0, :]
    by_row = rpb.astype(F32)[:, :, ri, :]
    by_row = by_row.reshape(depth, H_C // 2, 2, 5, 2, NA_BAND, ncol)
    by_row = jnp.transpose(by_row, (0, 3, 1, 5, 2, 4, 6))
    ea = jnp.arange(4)
    same_ea = (ea[None, :, None, None, None] == ea[None, None, None, :, None])
    hit = (jnp.arange(ncol)[None, None, :, None, None] == ci.T[:, None, None, None, :])
    onehot = (same_ea & hit).astype(F32).reshape(GRID_W, 4 * ncol, 2 * NA_QBLK)
    bias = jnp.einsum("rj,kjc->rkc", by_row.reshape(-1, 4 * ncol), onehot, precision=lax.Precision.HIGHEST)
    bias = bias.reshape(depth, 5, H_C // 2, NA_BAND * GRID_W, 2 * NA_QBLK)
    mask = jnp.transpose(mask, (0, 3, 4, 1, 2)).reshape(5, NA_BAND * GRID_W, NA_QBLK)
    mask = jnp.concatenate([mask, mask], axis=-1)
    return jnp.where(mask[None, :, None], bias * LOG2E, -jnp.inf)


def _prepare(t5_table, pre_norm_g, w_in, subln_g, q_norm_g, k_norm_g, na_rpb, w_branch_a, w_branch_b,
             w_branch_c, w_out, post_norm_g, w_ple_proj, w_ple_gate, seq, tq):
    depth = w_in.shape[0]
    two = lambda a: jnp.concatenate([a, a], axis=-1).reshape(depth, 1, LANES)
    return dict(
        w_in=_permute_in_proj(w_in).astype(BF16),
        pre_g=pre_norm_g.reshape(depth, 1, D_MODEL),
        qn=two(q_norm_g), kn=two(k_norm_g),
        subln=subln_g.reshape(depth, 1, LANES),
        wa=w_branch_a.astype(BF16), wb=_pair_major_heads(w_branch_b, 1).astype(BF16), wc=w_branch_c.astype(BF16),
        wo=w_out.astype(BF16), wpg=w_ple_gate.astype(BF16), wpe=w_ple_proj.astype(BF16),
        post_g=post_norm_g.reshape(depth, 1, D_MODEL),
        t5=_t5_tiles(t5_table, seq, tq, T5_TK), rope=_rope_tables(seq), na=_na_tables(na_rpb, seq),
    )


def _trunk(x, p, lambda_qk, w, *, tm, tq_a, tq_b, tq_c):
    b, seq, _ = x.shape
    depth = p.shape[0]
    n = b * seq
    xf = x.reshape(n, D_MODEL)
    pf = p.reshape(depth, n, D_PLE)
    for layer in range(depth):
        u, vta, vtb, vtc = _in_proj(xf, w["pre_g"], w["w_in"], w["qn"], w["kn"], w["rope"], layer, seq, tm)
        u3 = u.reshape(b, seq, U_W)
        ya = _diff_attn(u3, vta, lambda_qk, w["subln"], w["t5"], layer, tq_a)
        yb = _gqa_attn(u3, vtb, tq_b)
        yc = _na_attn(u3, vtc, w["na"], layer, tq_c)
        xf = _merge(xf, ya.reshape(n, W_BR), yb.reshape(n, W_BR), yc.reshape(n, W_BR), u, pf,
                    w["wa"], w["wb"], w["wc"], w["wo"], w["wpg"], w["wpe"], w["post_g"], layer, tm)
    return xf.reshape(b, seq, D_MODEL)


TILES = dict(tm=512, tq_a=512, tq_b=512, tq_c=1024)


def kernel(x_prompt, x_sample, p_prompt, p_sample, t5_table, pre_norm_g, w_in, lambda_qk, subln_g, q_norm_g,
           k_norm_g, na_rpb, w_branch_a, w_branch_b, w_branch_c, w_out, post_norm_g, w_ple_proj, w_ple_gate):
    assert x_prompt.shape[1] == x_sample.shape[1]
    w = _prepare(t5_table, pre_norm_g, w_in, subln_g, q_norm_g, k_norm_g, na_rpb, w_branch_a, w_branch_b,
                 w_branch_c, w_out, post_norm_g, w_ple_proj, w_ple_gate, x_prompt.shape[1], TILES["tq_a"])
    y_prompt = _trunk(x_prompt, p_prompt, lambda_qk, w, **TILES)
    y_sample = _trunk(x_sample, p_sample, lambda_qk, w, **TILES)
    return (y_prompt, y_sample)
```

```python
import functools
import math

import jax
import jax.numpy as jnp
from jax import lax
from jax.experimental import pallas as pl
from jax.experimental.pallas import tpu as pltpu

F32 = jnp.float32
BF16 = jnp.bfloat16

D_MODEL = 1024
D_PLE = 256
EPS = 1e-6
GRID_W = 64
LANES = 128
SUBLANES = 8
ONES_ROWS = 16
DH = 64
H_A = 4
H_B = 8
KV_B = 2
REP_B = H_B // KV_B
H_C = 8
W_BR = 512
NA_ROWS = 8
NA_COLS = 16
NA_BAND = 10
NA_QBLK = 2 * GRID_W
T5_BUCKETS = 32
T5_MAX_DIST = 128
ATTN_TK = 256
T5_TK = ATTN_TK
ROPE_THETA = 10000.0
LOG2E = math.log2(math.e)
QK_SCALE = DH ** -0.5 * LOG2E

OFF_MG = 0
OFF_AQ, OFF_AK, OFF_AG = 3072, 3584, 4096
OFF_BQ, OFF_BG = 4608, 5120
OFF_CQ, OFF_CK, OFF_CG = 5632, 6144, 6656
OFF_BK = 7168
U_W = OFF_BK + LANES
W_AV, W_CV, W_BKV = 7168, 7680, 8192
IN_W = 8448

VMEM_LIMIT = 56 * 1024 * 1024


def _pair_major_heads(w, axis):
    shp = w.shape
    w = w.reshape(shp[:axis] + (KV_B, REP_B, DH) + shp[axis + 1:])
    return jnp.swapaxes(w, axis, axis + 1).reshape(shp)


def _permute_in_proj(w_in):
    sizes = dict(aq=512, ak=512, av=512, ag=512, bq=512, bk=128, bv=128, bg=512, cq=512, ck=512, cv=512,
                 cg=512, mg=3072)
    seg, off = {}, 0
    for name, size in sizes.items():
        seg[name] = w_in[:, :, off:off + size]
        off += size
    seg["aq"] = seg["aq"] * QK_SCALE
    seg["cq"] = seg["cq"] * QK_SCALE
    seg["bq"] = _pair_major_heads(seg["bq"], 2)
    seg["bg"] = _pair_major_heads(seg["bg"], 2)
    order = ("mg", "aq", "ak", "ag", "bq", "bg", "cq", "ck", "cg", "av", "cv", "bk", "bv")
    return jnp.concatenate([seg[n] for n in order], axis=2)


def _silu(x):
    return x * jax.nn.sigmoid(x)


def _dot_nt(a, b):
    return lax.dot_general(a, b, (((1,), (1,)), ((), ())), preferred_element_type=F32)


def _dot(a, b):
    return jnp.dot(a, b, preferred_element_type=F32)


def _pair_norm_rope(a, gain, cos, sin_lo, sin_hi, left):
    sq = a * a
    s_l = jnp.sum(jnp.where(left, sq, 0.0), axis=-1, keepdims=True)
    s_r = jnp.sum(jnp.where(left, 0.0, sq), axis=-1, keepdims=True)
    ms = jnp.where(left, s_l, s_r) * (1.0 / DH)
    y = a * lax.rsqrt(ms + EPS) * gain
    return y * cos + pltpu.roll(y, 112, 1) * sin_lo + pltpu.roll(y, 16, 1) * sin_hi


def _in_proj_kernel(x_ref, g_ref, w_ref, qn_ref, kn_ref, cos_ref, slo_ref, shi_ref,
                    u_ref, vta_ref, vtb_ref, vtc_ref, qta_ref, qtb_ref):
    x = x_ref[...]
    ms = jnp.mean(x * x, axis=-1, keepdims=True)
    h = (x * lax.rsqrt(ms + EPS) * g_ref[...]).astype(BF16)
    tm = x.shape[0]
    left = lax.broadcasted_iota(jnp.int32, (tm, LANES), 1) < DH
    cos, slo, shi = cos_ref[...], slo_ref[...], shi_ref[...]
    acc = _dot(h, w_ref[:, W_BKV:IN_W])
    bk = _pair_norm_rope(acc[:, :LANES], kn_ref[...], cos, slo, shi, left)
    u_ref[:, OFF_BK:U_W] = bk.astype(BF16)
    vtb_ref[0] = acc[:, LANES:].T.astype(BF16)
    acc = _dot(h, w_ref[:, OFF_BQ:OFF_BQ + W_BR])
    for b in range(W_BR // LANES):
        blk = _pair_norm_rope(acc[:, b * LANES:(b + 1) * LANES], qn_ref[...], cos, slo, shi, left)
        blk = blk * QK_SCALE
        u_ref[:, OFF_BQ + b * LANES:OFF_BQ + (b + 1) * LANES] = blk.astype(BF16)
        qtb_ref[b] = blk.T.astype(BF16)
    for c0, vt_ref in ((W_AV, vta_ref), (W_CV, vtc_ref)):
        vt = _dot(h, w_ref[:, c0:c0 + W_BR]).T
        vt_ref[...] = vt.reshape(W_BR // LANES, LANES, tm).astype(BF16)
    acc = _dot(h, w_ref[:, OFF_AQ:OFF_AQ + W_BR])
    u_ref[:, OFF_AQ:OFF_AQ + W_BR] = acc.astype(BF16)
    qta_ref[...] = acc.T.reshape(W_BR // LANES, LANES, tm).astype(BF16)
    for c0 in range(0, OFF_BK, W_BR):
        if c0 not in (OFF_BQ, OFF_AQ):
            u_ref[:, c0:c0 + W_BR] = _dot(h, w_ref[:, c0:c0 + W_BR]).astype(BF16)


def _in_proj(x, pre_g, w_in, qn, kn, rope, layer, seq, tm):
    n = x.shape[0]
    nseq = seq // tm
    cos, slo, shi = rope
    vec = lambda width: pl.BlockSpec((None, 1, width), lambda t: (layer, 0, 0))
    tab = pl.BlockSpec((tm, LANES), lambda t: (t % nseq, 0))
    vt_spec = lambda heads: pl.BlockSpec((None, heads, LANES, tm), lambda t: (t // nseq, 0, 0, t % nseq))
    vt_shape = lambda heads: jax.ShapeDtypeStruct((n // seq, heads, LANES, seq), BF16)
    return pl.pallas_call(
        _in_proj_kernel,
        grid=(n // tm,),
        in_specs=[
            pl.BlockSpec((tm, D_MODEL), lambda t: (t, 0)),
            vec(D_MODEL),
            pl.BlockSpec((None, D_MODEL, IN_W), lambda t: (layer, 0, 0), pipeline_mode=pl.Buffered(1)),
            vec(LANES), vec(LANES), tab, tab, tab,
        ],
        out_specs=[pl.BlockSpec((tm, U_W), lambda t: (t, 0)), vt_spec(H_A), vt_spec(1), vt_spec(H_C // 2),
                   vt_spec(H_A), vt_spec(REP_B)],
        out_shape=[jax.ShapeDtypeStruct((n, U_W), BF16), vt_shape(H_A), vt_shape(1), vt_shape(H_C // 2),
                   vt_shape(H_A), vt_shape(REP_B)],
        compiler_params=pltpu.CompilerParams(dimension_semantics=("arbitrary",),
                                             vmem_limit_bytes=VMEM_LIMIT),
        name="in_proj",
    )(x, pre_g, w_in, qn, kn, cos, slo, shi)


def _chunk(c, tk):
    if isinstance(c, int):
        return slice(c * tk, (c + 1) * tk)
    return pl.ds(pl.multiple_of(c * tk, tk), tk)


class _T5Bias:
    def __init__(self, t_ref, cst_ref, base_w, base_r, per_q, nchunk):
        self.t_ref, self.cst_ref = t_ref, cst_ref
        self.base_w, self.base_r = base_w, base_r
        self.per_q, self.nchunk = per_q, nchunk

    def _tiled(self, n):
        return n <= self.per_q + 1

    def chunks(self, n):
        return (self.base_r + n - 1) % self.nchunk, (self.base_w + n - 1) % self.nchunk

    def add_tile(self, n, cw, mp, s):
        if not self._tiled(n):
            return s
        return s + self.t_ref[jnp.clip(cw - self.base_w, -2, self.per_q + 1) + 2, mp]

    def add_constant(self, n, cw, mp, m_c):
        if self._tiled(n):
            return m_c
        side = jnp.where(cw > self.base_w, self.t_ref.shape[0] - 1, 0)
        return m_c + self.t_ref[side, mp, 0:SUBLANES, :]

    def shifted_max(self, n, cr, mp, m):
        if self._tiled(n):
            return m
        return m - self.cst_ref[mp, (cr > self.base_r).astype(jnp.int32)]


def _two_stage_scratch(tq, seq, tk):
    return [pltpu.VMEM((2, seq // tk, tk, tq), F32), pltpu.VMEM((2, SUBLANES, tq), F32)] * 2


def _two_stage_step(qs, k_ref, vt_ref, bias, s_w, m_w, s_r, m_r):
    _, nchunk, tk, tq = s_w.shape
    slabs = tk // SUBLANES
    m_run, acc = [None] * 2, [None] * 2
    for n in range(nchunk):
        cr, cw = (n, n) if bias is None else bias.chunks(n)
        kc = k_ref[_chunk(cw, tk), :]
        vtc = _with_ones(vt_ref[:, _chunk(cr, tk)])
        for e in range(2):
            m_prev = m_r[e] if bias is None else bias.shifted_max(n, cr, e, m_r[e])
            p = jnp.exp2(s_r[e, cr].reshape(slabs, SUBLANES, tq) - m_prev[None])
            pv = _dot(vtc, p.reshape(tk, tq).astype(BF16))
            acc[e] = pv if n == 0 else acc[e] + pv
            s = _dot(kc, qs[e])
            if bias is not None:
                s = bias.add_tile(n, cw, e, s)
            s_w[e, cw] = s
            m_c = jnp.max(s.reshape(slabs, SUBLANES, tq), axis=0)
            if bias is not None:
                m_c = bias.add_constant(n, cw, e, m_c)
            m_run[e] = m_c if n == 0 else jnp.maximum(m_run[e], m_c)
    outs = []
    for e in range(2):
        m_w[e] = jnp.broadcast_to(jnp.max(m_run[e], axis=0, keepdims=True), (SUBLANES, tq))
        outs.append((acc[e][:LANES] / acc[e][LANES:LANES + 1]).T)
    return outs


def _two_stage(step, scratch):
    s_a, m_a, s_b, m_b = scratch
    t = pl.program_id(0)

    @pl.when(t == 0)
    def _():
        s_b[...] = jnp.zeros_like(s_b)
        m_b[...] = jnp.zeros_like(m_b)

    @pl.when(t % 2 == 0)
    def _():
        step(s_a, m_a, s_b, m_b)

    @pl.when(t % 2 == 1)
    def _():
        step(s_b, m_b, s_a, m_a)


def _row_halves(qt):
    top = lax.broadcasted_iota(jnp.int32, qt.shape, 0) < DH
    zero = jnp.zeros_like(qt)
    return jnp.where(top, qt, zero), jnp.where(top, zero, qt)


def _lane_halves(q):
    left = lax.broadcasted_iota(jnp.int32, q.shape, 1) < DH
    zero = jnp.zeros_like(q)
    return jnp.where(left, q, zero), jnp.where(left, zero, q)


def _unit_maps(b, heads, n_i):
    units = b * heads * n_i

    def unit(t, lag):
        un = jnp.clip(t - lag, 0, units - 1)
        return un // (heads * n_i), (un // n_i) % heads, un % n_i

    return units, unit


def _diff_attn_kernel(lam_ref, sg_ref, q_ref, k_ref, vt_ref, g_ref, t_ref, cst_ref, o_ref, *scratch,
                      lambda_init, n_i, units):
    t = pl.program_id(0)
    per_q = q_ref.shape[1] // t_ref.shape[2]
    i_w = jnp.minimum(t, units - 1) % n_i
    i_r = jnp.clip(t - 1, 0, units - 1) % n_i
    bias = _T5Bias(t_ref, cst_ref, i_w * per_q, i_r * per_q, per_q, k_ref.shape[0] // t_ref.shape[2])

    def step(*bufs):
        lp = lam_ref[...]
        lam = (jnp.exp(jnp.sum(lp[0:1] * lp[1:2], keepdims=True))
               - jnp.exp(jnp.sum(lp[2:3] * lp[3:4], keepdims=True)) + lambda_init)
        o0, o1 = _two_stage_step(_row_halves(q_ref[...]), k_ref, vt_ref, bias, *bufs)
        o = o0 - lam * o1
        o = o * lax.rsqrt(jnp.mean(o * o, axis=-1, keepdims=True) + EPS) * sg_ref[...]
        o = o * (1.0 - lambda_init)
        o_ref[...] = (o * _silu(g_ref[...].astype(F32))).astype(BF16)

    _two_stage(step, scratch)


def _with_ones(vt):
    return jnp.concatenate([vt, jnp.ones((ONES_ROWS, vt.shape[1]), vt.dtype)], axis=0)


def _diff_attn(u, qt, vt, lam_qk, subln_g, t5_tiles, layer, tq):
    b, seq, _ = u.shape
    ntile, _, tk, _ = t5_tiles.shape
    n_i = seq // tq
    units, unit = _unit_maps(b, H_A, n_i)
    lambda_init = 0.8 - 0.6 * math.exp(-0.3 * layer)
    far_bias = jnp.stack([t5_tiles[0, :, :SUBLANES], t5_tiles[-1, :, :SUBLANES]], axis=1)
    far_bias = far_bias.reshape(H_A, 2, 2, SUBLANES, tq)

    def row_blk(off, lag):
        def index(t):
            bi, h, i = unit(t, lag)
            return bi, i, off // LANES + h
        return pl.BlockSpec((None, tq, LANES), index)

    def seq_blk(off, lag):
        def index(t):
            bi, h, _ = unit(t, lag)
            return bi, 0, off // LANES + h
        return pl.BlockSpec((None, seq, LANES), index)

    return pl.pallas_call(
        functools.partial(_diff_attn_kernel, lambda_init=lambda_init, n_i=n_i, units=units),
        grid=(units + 1,),
        in_specs=[
            pl.BlockSpec((None, 4, DH), lambda t: (layer, 0, 0)),
            pl.BlockSpec((None, 1, LANES), lambda t: (layer, 0, 0)),
            pl.BlockSpec((None, None, LANES, tq), lambda t: unit(t, 0)[:2] + (0, unit(t, 0)[2])),
            seq_blk(OFF_AK, 0),
            pl.BlockSpec((None, None, LANES, seq), lambda t: unit(t, 1)[:2] + (0, 0)),
            row_blk(OFF_AG, 1),
            pl.BlockSpec((ntile, 2, tk, tq), lambda t: (0, unit(t, 0)[1], 0, 0)),
            pl.BlockSpec((None, 2, 2, SUBLANES, tq), lambda t: (unit(t, 1)[1], 0, 0, 0, 0)),
        ],
        out_specs=row_blk(0, 1),
        out_shape=jax.ShapeDtypeStruct((b, seq, W_BR), BF16),
        scratch_shapes=_two_stage_scratch(tq, seq, tk),
        compiler_params=pltpu.CompilerParams(dimension_semantics=("arbitrary",),
                                             vmem_limit_bytes=VMEM_LIMIT),
        name="diff_attn",
    )(lam_qk, subln_g, qt, u, vt, u, t5_tiles, far_bias)


def _gqa_kernel(q_ref, k_ref, vt_ref, g_ref, o_ref, *scratch):
    def step(*bufs):
        o0, o1 = _two_stage_step(_row_halves(q_ref[...]), k_ref, vt_ref, None, *bufs)
        left = lax.broadcasted_iota(jnp.int32, o0.shape, 1) < DH
        o = jnp.where(left, o0, o1)
        o_ref[...] = (o * _silu(g_ref[...].astype(F32))).astype(BF16)

    _two_stage(step, scratch)


def _gqa_attn(u, qt, vt, tq):
    b, seq, _ = u.shape
    units, unit = _unit_maps(b, REP_B, seq // tq)

    def row_blk(off, lag):
        def index(t):
            bi, pr, i = unit(t, lag)
            return bi, i, off // LANES + pr
        return pl.BlockSpec((None, tq, LANES), index)

    def seq_blk(off, lag):
        return pl.BlockSpec((None, seq, LANES), lambda t: (unit(t, lag)[0], 0, off // LANES))

    return pl.pallas_call(
        _gqa_kernel,
        grid=(units + 1,),
        in_specs=[
            pl.BlockSpec((None, None, LANES, tq), lambda t: unit(t, 0)[:2] + (0, unit(t, 0)[2])),
            seq_blk(OFF_BK, 0),
            pl.BlockSpec((None, None, LANES, seq), lambda t: (unit(t, 1)[0], 0, 0, 0)),
            row_blk(OFF_BG, 1),
        ],
        out_specs=row_blk(0, 1),
        out_shape=jax.ShapeDtypeStruct((b, seq, W_BR), BF16),
        scratch_shapes=_two_stage_scratch(tq, seq, ATTN_TK),
        compiler_params=pltpu.CompilerParams(dimension_semantics=("arbitrary",),
                                             vmem_limit_bytes=VMEM_LIMIT),
        name="gqa_attn",
    )(qt, u, vt, u)


def _na_block_type(bi, nblk):
    return jnp.where(bi < 2, bi, jnp.where(bi <= nblk - 3, 2, bi - (nblk - 5)))


def _na_kernel(q_ref, k_ref, vt_ref, g_ref, tab_ref, o_ref, *, rows):
    big = pl.program_id(1)
    nsub = q_ref.shape[0] // NA_QBLK
    nblk = rows // 2
    nkeys = NA_BAND * GRID_W
    left = lax.broadcasted_iota(jnp.int32, (NA_QBLK, LANES), 1) < DH

    def slices(sub, pair):
        bi = big * nsub + sub
        b0 = jnp.clip(2 * bi - NA_ROWS // 2, 0, rows - NA_BAND)
        ks = pl.ds(pl.multiple_of(b0 * GRID_W, 2 * GRID_W), nkeys)
        return (_na_block_type(bi, nblk), ks, slice(sub * NA_QBLK, (sub + 1) * NA_QBLK),
                slice(pair * LANES, (pair + 1) * LANES))

    def scores(sub, pair):
        typ, ks, rs, cs = slices(sub, pair)
        qq = jnp.concatenate(_lane_halves(q_ref[rs, cs]), axis=0)
        return _dot_nt(k_ref[ks, cs], qq) + tab_ref[typ, pair]

    def weights(s):
        return jnp.exp2(s - jnp.max(s, axis=0, keepdims=True)).astype(BF16)

    def finish(sub, pair, p):
        _, ks, rs, cs = slices(sub, pair)
        ot = _dot(_with_ones(vt_ref[pair, :, ks]), p)
        o = (ot[:LANES] / ot[LANES:LANES + 1]).T
        o = jnp.where(left, o[:NA_QBLK], o[NA_QBLK:])
        o_ref[rs, cs] = (o * _silu(g_ref[rs, cs].astype(F32))).astype(BF16)

    units = [(sub, pair) for sub in range(nsub) for pair in range(H_C // 2)]
    s_prev, p_prev = None, None
    for n in range(len(units) + 2):
        s_new = scores(*units[n]) if n < len(units) else None
        p_new = weights(s_prev) if s_prev is not None else None
        if p_prev is not None:
            finish(*units[n - 2], p_prev)
        s_prev, p_prev = s_new, p_new


def _na_attn(u, vt, na_tab, layer, tq):
    b, seq, _ = u.shape
    rows = seq // GRID_W
    assert (rows - NA_BAND) % 2 == 0
    row_blk = lambda off: pl.BlockSpec((None, tq, W_BR), lambda bi, i: (bi, i, off // W_BR))
    seq_blk = lambda off: pl.BlockSpec((None, seq, W_BR), lambda bi, i: (bi, 0, off // W_BR))
    return pl.pallas_call(
        functools.partial(_na_kernel, rows=rows),
        grid=(b, seq // tq),
        in_specs=[
            row_blk(OFF_CQ), seq_blk(OFF_CK),
            pl.BlockSpec((None,) + vt.shape[1:], lambda bi, i: (bi, 0, 0, 0)),
            row_blk(OFF_CG),
            pl.BlockSpec((None,) + na_tab.shape[1:], lambda bi, i: (layer, 0, 0, 0, 0),
                         pipeline_mode=pl.Buffered(1)),
        ],
        out_specs=pl.BlockSpec((None, tq, W_BR), lambda bi, i: (bi, i, 0)),
        out_shape=jax.ShapeDtypeStruct((b, seq, W_BR), BF16),
        compiler_params=pltpu.CompilerParams(dimension_semantics=("arbitrary", "arbitrary"),
                                             vmem_limit_bytes=VMEM_LIMIT),
        name="nbr_attn",
    )(u, u, vt, u, na_tab)


def _merge_kernel(x_ref, ya_ref, yb_ref, yc_ref, mg_ref, p_ref, wa_ref, wb_ref, wc_ref, wo_ref,
                  wpg_ref, wpe_ref, pg_ref, o_ref):
    def gate(k):
        return jax.nn.sigmoid(mg_ref[:, k * D_MODEL:(k + 1) * D_MODEL].astype(F32))

    m = gate(0) * _dot(ya_ref[...], wa_ref[...])
    m = m + gate(1) * _dot(yb_ref[...], wb_ref[...])
    m = m + gate(2) * _dot(yc_ref[...], wc_ref[...])
    r = _dot(m.astype(BF16), wo_ref[...])
    r = r * lax.rsqrt(jnp.mean(r * r, axis=-1, keepdims=True) + EPS) * pg_ref[...]
    x = x_ref[...] + r
    emb = _dot(p_ref[...].astype(BF16), wpe_ref[...])
    o_ref[...] = x + jax.nn.sigmoid(_dot(x.astype(BF16), wpg_ref[...])) * emb


def _merge(x, ya, yb, yc, u, p, wa, wb, wc, wo, wpg, wpe, post_g, layer, tm):
    n = x.shape[0]
    tok = lambda width: pl.BlockSpec((tm, width), lambda t: (t, 0))
    wgt = lambda r, c: pl.BlockSpec((None, r, c), lambda t: (layer, 0, 0))
    return pl.pallas_call(
        _merge_kernel,
        grid=(n // tm,),
        in_specs=[
            tok(D_MODEL), tok(W_BR), tok(W_BR), tok(W_BR),
            pl.BlockSpec((tm, 3 * D_MODEL), lambda t: (t, OFF_MG // (3 * D_MODEL))),
            pl.BlockSpec((None, tm, D_PLE), lambda t: (layer, t, 0)),
            wgt(W_BR, D_MODEL), wgt(W_BR, D_MODEL), wgt(W_BR, D_MODEL),
            wgt(D_MODEL, D_MODEL), wgt(D_MODEL, D_MODEL), wgt(D_PLE, D_MODEL),
            wgt(1, D_MODEL),
        ],
        out_specs=tok(D_MODEL),
        out_shape=jax.ShapeDtypeStruct((n, D_MODEL), F32),
        compiler_params=pltpu.CompilerParams(dimension_semantics=("arbitrary",),
                                             vmem_limit_bytes=VMEM_LIMIT),
        name="merge",
    )(x, ya, yb, yc, u, p, wa, wb, wc, wo, wpg, wpe, post_g)


def _t5_bucket(rel):
    nb = T5_BUCKETS // 2
    ret = jnp.where(rel > 0, nb, 0)
    n = jnp.abs(rel)
    max_exact = nb // 2
    nf = jnp.maximum(n, 1).astype(F32)
    large = max_exact + (jnp.log(nf / max_exact) / math.log(T5_MAX_DIST / max_exact)
                         * (nb - max_exact)).astype(jnp.int32)
    large = jnp.minimum(large, nb - 1)
    return ret + jnp.where(n < max_exact, n, large)


def _t5_tiles(t5_table, seq, tq, tk):
    assert tk >= T5_MAX_DIST and tq % tk == 0
    rel = jnp.arange(-(seq - 1), seq)
    rel_bias = t5_table[_t5_bucket(rel)].astype(F32).T * LOG2E
    big = tq + (tq // tk + 2) * tk
    padded = jnp.pad(rel_bias, ((0, 0), (big, big)), mode="edge")
    period = tq + tk
    tiles = []
    for e in range(-2, tq // tk + 2):
        base = seq - 1 + big + e * tk
        rev = padded[:, base - tq:base + tk][:, ::-1]
        w = jnp.concatenate([rev[:, tk - 1:], rev[:, :tk - 1]], axis=1)
        toe = jnp.tile(w, (1, tk))[:, :tk * (period - 1)].reshape(-1, tk, period - 1)
        tiles.append(toe[:, :, :tq])
    return jnp.stack(tiles)


def _rope_tables(seq):
    t = jnp.arange(seq)
    row = (t // GRID_W).astype(F32)
    col = (t % GRID_W).astype(F32)
    n_freq = DH // 4
    inv = ROPE_THETA ** (-jnp.arange(n_freq, dtype=F32) / n_freq)
    ang = jnp.concatenate([row[:, None] * inv] * 2 + [col[:, None] * inv] * 2, axis=1)
    cos, sin = jnp.cos(ang), jnp.sin(ang)
    first_half = (jnp.arange(DH) % (2 * n_freq)) < n_freq
    sin_lo = jnp.where(first_half, -sin, 0.0)
    sin_hi = jnp.where(first_half, 0.0, sin)
    two = lambda a: jnp.concatenate([a, a], axis=1)
    return two(cos), two(sin_lo), two(sin_hi)


def _na_tables(rpb, seq):
    rows = seq // GRID_W
    nblk = rows // 2
    assert rows >= NA_BAND and nblk >= 5
    wr = min(NA_ROWS, rows)
    depth = rpb.shape[0]
    ncol = 2 * NA_COLS - 1
    bis = jnp.array([0, 1, 2, nblk - 2, nblk - 1])[:, None, None, None, None]
    qr = 2 * bis + jnp.arange(2)[None, :, None, None, None]
    qc = jnp.arange(GRID_W)[None, None, :, None, None]
    kr = (jnp.clip(2 * bis - NA_ROWS // 2, 0, rows - NA_BAND)
          + jnp.arange(NA_BAND)[None, None, None, :, None])
    kc = jnp.arange(GRID_W)[None, None, None, None, :]
    sr = jnp.clip(qr - wr // 2, 0, rows - wr)
    sc = jnp.clip(qc - NA_COLS // 2, 0, GRID_W - NA_COLS)
    mask = (kr >= sr) & (kr < sr + wr) & (kc >= sc) & (kc < sc + NA_COLS)
    ri = jnp.clip(kr - qr + NA_ROWS - 1, 0, 2 * NA_ROWS - 2)[:, :, 0, :, 0]
    ci = jnp.clip(kc - qc + NA_COLS - 1, 0, ncol - 1)[0, 0, :, 0, :]
    by_row = rpb.astype(F32)[:, :, ri, :]
    by_row = by_row.reshape(depth, H_C // 2, 2, 5, 2, NA_BAND, ncol)
    by_row = jnp.transpose(by_row, (0, 3, 1, 5, 2, 4, 6))
    ea = jnp.arange(4)
    same_ea = (ea[None, :, None, None, None] == ea[None, None, None, :, None])
    hit = (jnp.arange(ncol)[None, None, :, None, None] == ci.T[:, None, None, None, :])
    onehot = (same_ea & hit).astype(F32).reshape(GRID_W, 4 * ncol, 2 * NA_QBLK)
    bias = jnp.einsum("rj,kjc->rkc", by_row.reshape(-1, 4 * ncol), onehot, precision=lax.Precision.HIGHEST)
    bias = bias.reshape(depth, 5, H_C // 2, NA_BAND * GRID_W, 2 * NA_QBLK)
    mask = jnp.transpose(mask, (0, 3, 4, 1, 2)).reshape(5, NA_BAND * GRID_W, NA_QBLK)
    mask = jnp.concatenate([mask, mask], axis=-1)
    return jnp.where(mask[None, :, None], bias * LOG2E, -jnp.inf)


def _prepare(t5_table, pre_norm_g, w_in, subln_g, q_norm_g, k_norm_g, na_rpb, w_branch_a, w_branch_b,
             w_branch_c, w_out, post_norm_g, w_ple_proj, w_ple_gate, seq, tq):
    depth = w_in.shape[0]
    two = lambda a: jnp.concatenate([a, a], axis=-1).reshape(depth, 1, LANES)
    return dict(
        w_in=_permute_in_proj(w_in).astype(BF16),
        pre_g=pre_norm_g.reshape(depth, 1, D_MODEL),
        qn=two(q_norm_g), kn=two(k_norm_g),
        subln=subln_g.reshape(depth, 1, LANES),
        wa=w_branch_a.astype(BF16), wb=_pair_major_heads(w_branch_b, 1).astype(BF16), wc=w_branch_c.astype(BF16),
        wo=w_out.astype(BF16), wpg=w_ple_gate.astype(BF16), wpe=w_ple_proj.astype(BF16),
        post_g=post_norm_g.reshape(depth, 1, D_MODEL),
        t5=_t5_tiles(t5_table, seq, tq, T5_TK), rope=_rope_tables(seq), na=_na_tables(na_rpb, seq),
    )


def _trunk(x, p, lambda_qk, w, *, tm, tq_a, tq_b, tq_c):
    b, seq, _ = x.shape
    depth = p.shape[0]
    n = b * seq
    xf = x.reshape(n, D_MODEL)
    pf = p.reshape(depth, n, D_PLE)
    for layer in range(depth):
        u, vta, vtb, vtc, qta, qtb = _in_proj(xf, w["pre_g"], w["w_in"], w["qn"], w["kn"], w["rope"], layer,
                                              seq, tm)
        u3 = u.reshape(b, seq, U_W)
        ya = _diff_attn(u3, qta, vta, lambda_qk, w["subln"], w["t5"], layer, tq_a)
        yb = _gqa_attn(u3, qtb, vtb, tq_b)
        yc = _na_attn(u3, vtc, w["na"], layer, tq_c)
        xf = _merge(xf, ya.reshape(n, W_BR), yb.reshape(n, W_BR), yc.reshape(n, W_BR), u, pf,
                    w["wa"], w["wb"], w["wc"], w["wo"], w["wpg"], w["wpe"], w["post_g"], layer, tm)
    return xf.reshape(b, seq, D_MODEL)


TILES = dict(tm=512, tq_a=512, tq_b=512, tq_c=1024)


def kernel(x_prompt, x_sample, p_prompt, p_sample, t5_table, pre_norm_g, w_in, lambda_qk, subln_g, q_norm_g,
           k_norm_g, na_rpb, w_branch_a, w_branch_b, w_branch_c, w_out, post_norm_g, w_ple_proj, w_ple_gate):
    assert x_prompt.shape[1] == x_sample.shape[1]
    w = _prepare(t5_table, pre_norm_g, w_in, subln_g, q_norm_g, k_norm_g, na_rpb, w_branch_a, w_branch_b,
                 w_branch_c, w_out, post_norm_g, w_ple_proj, w_ple_gate, x_prompt.shape[1], TILES["tq_a"])
    y_prompt = _trunk(x_prompt, p_prompt, lambda_qk, w, **TILES)
    y_sample = _trunk(x_sample, p_sample, lambda_qk, w, **TILES)
    return (y_prompt, y_sample)
```
